```python
import numpy as np
import jax, jax.numpy as jnp
from jax import lax

D_MODEL = 4096
BATCH = 16
SEQ = 256
DEPTH = 2
DEC_BATCH = 8
DEC_SEQ = 4096
PAST_LEN = 256

GRID_W = 64
HEAD_DIM = 128
D_MIX = D_MODEL
H_RET = D_MIX // 4 // HEAD_DIM
D_RET = H_RET * HEAD_DIM
SM_GROUPS = 8
SM_GROUP_DIM = D_MIX // 4 // SM_GROUPS
D_SM = SM_GROUPS * SM_GROUP_DIM
H_ATT = D_MIX // 2 // HEAD_DIM
KV_HEADS = 4
D_ATT = H_ATT * HEAD_DIM
D_KV = KV_HEADS * HEAD_DIM
CHUNK = 128
Q_BLOCK = 128
ROPE_AXIS_DIM = HEAD_DIM // 2
ROPE_PAIRS = ROPE_AXIS_DIM // 2
ROPE_THETA = 10000.0
N_GROUPS = 4
EXPERTS_PER_GROUP = 8
N_EXPERTS = N_GROUPS * EXPERTS_PER_GROUP
TOP_K = 2
D_EXPERT = D_MODEL // 4
MOE_BLOCK = 128
N_MOD = 6
IN_SIZES = (D_RET, D_RET, D_RET, D_RET, D_SM, D_SM, D_ATT, D_KV, D_KV)
N_IN = D_RET * 4 + D_SM * 2 + D_ATT + D_KV * 2
DEEPNORM_ALPHA = (2 * DEPTH) ** 0.25
DEEPNORM_BETA = (8 * DEPTH) ** -0.25
LN_EPS = 1e-5
RMS_EPS = 1e-6

kernel_name = 'hybrid_ret_gmlp_gqa_hmoe_diffusion_step'

F32 = jnp.float32


def _normalize(x):
    xf = x.astype(F32)
    mu = jnp.mean(xf, axis=-1, keepdims=True)
    var = jnp.mean(jnp.square(xf - mu), axis=-1, keepdims=True)
    return (xf - mu) * lax.rsqrt(var + LN_EPS)


def layer_norm(x, w, b):
    return (_normalize(x) * w + b).astype(x.dtype)


def rms_norm(x, w):
    xf = x.astype(F32)
    return (xf * lax.rsqrt(jnp.mean(xf * xf, axis=-1, keepdims=True) + RMS_EPS) * w).astype(x.dtype)


def adaln_mods(cond, w_mod, b_mod):
    return (jax.nn.silu(cond) @ w_mod + b_mod).reshape(cond.shape[0], N_MOD, D_MODEL)


def axial_rope_tables(n_tokens):
    rows = n_tokens // GRID_W
    row = jnp.broadcast_to(jnp.arange(rows, dtype=F32)[:, None], (rows, GRID_W)).reshape(-1)
    col = jnp.broadcast_to(jnp.arange(GRID_W, dtype=F32)[None, :], (rows, GRID_W)).reshape(-1)
    inv = ROPE_THETA ** (-2.0 * jnp.arange(ROPE_PAIRS, dtype=F32) / ROPE_AXIS_DIM)
    ang = jnp.stack([row[:, None] * inv, col[:, None] * inv], axis=1)
    return jnp.cos(ang), jnp.sin(ang)


def apply_axial_rope(x, cos, sin):
    B, L, H, d = x.shape
    xr = x.astype(F32).reshape(B, L, H, 2, 2, ROPE_PAIRS)
    a, b = xr[..., 0, :], xr[..., 1, :]
    c = cos[None, :, None]
    s = sin[None, :, None]
    out = jnp.stack([a * c - b * s, a * s + b * c], axis=-2)
    return out.reshape(B, L, H, d).astype(x.dtype)


def retention_scan(q, k, v, log_decay, s0):
    B, L, H, d = q.shape
    n = L // CHUNK
    lg = log_decay.astype(F32)
    pos = jnp.arange(CHUNK, dtype=F32)
    diff = pos[:, None] - pos[None, :]
    intra = jnp.where(diff >= 0, jnp.exp(lg[:, None, None] * jnp.maximum(diff, 0.0)), 0.0)
    q_dec = jnp.exp(lg[None, :] * (pos[:, None] + 1.0))
    k_dec = jnp.exp(lg[None, :] * (CHUNK - 1.0 - pos[:, None]))
    chunk_dec = jnp.exp(lg * CHUNK)

    def to_chunks(t):
        return t.astype(F32).reshape(B, n, CHUNK, H, d).transpose(1, 0, 2, 3, 4)

    def step(s, qkv):
        qc, kc, vc = qkv
        a = jnp.einsum('bihd,bjhd->bhij', qc, kc) * intra
        o = (jnp.einsum('bhij,bjhe->bihe', a, vc)
             + jnp.einsum('bihd,bhde->bihe', qc, s) * q_dec[None, :, :, None])
        s = (s * chunk_dec[None, :, None, None]
             + jnp.einsum('bjhd,bjhe->bhde', kc * k_dec[None, :, :, None], vc))
        return s, o

    s_fin, o = lax.scan(step, s0.astype(F32), (to_chunks(q), to_chunks(k * (d ** -0.5)), to_chunks(v)))
    return o.transpose(1, 0, 2, 3, 4).reshape(B, L, H, d), s_fin


def spatial_gating(u, v, ln_w, w_s, b_s):
    B, L, _ = u.shape
    n = L // CHUNK
    vn = (_normalize(v) * ln_w).reshape(B, n, CHUNK, SM_GROUPS, SM_GROUP_DIM)
    s = jnp.einsum('gij,bnjgc->bnigc', w_s.astype(F32), vn) + b_s.T.astype(F32)[None, None, :, :, None]
    return (u.astype(F32) * s.reshape(B, L, D_SM)).astype(u.dtype)


def blocked_attention(q, k, v):
    B, Lq, H, d = q.shape
    kvh = k.shape[2]
    g = H // kvh
    nb = Lq // Q_BLOCK
    qb = q.reshape(B, nb, Q_BLOCK, kvh, g, d).transpose(1, 0, 2, 3, 4, 5)
    scale = d ** -0.5

    def one_block(qblk):
        s = jnp.einsum('bqhgd,bkhd->bhgqk', qblk, k, preferred_element_type=F32) * scale
        p = jax.nn.softmax(s, axis=-1)
        return jnp.einsum('bhgqk,bkhd->bqhgd', p.astype(v.dtype), v)

    o = lax.map(one_block, qb)
    return o.transpose(1, 0, 2, 3, 4, 5).reshape(B, Lq, H * d)


def hier_moe(x2d, w_coarse, b_coarse, w_fine, b_fine, w_up, w_down):
    T, D = x2d.shape
    lc = jnp.dot(x2d, w_coarse, preferred_element_type=F32) + b_coarse.astype(F32)
    grp = jnp.argmax(lc, axis=-1).astype(jnp.int32)
    p_grp = jnp.take_along_axis(jax.nn.softmax(lc, axis=-1), grp[:, None], axis=-1)
    lf = (jnp.dot(x2d, w_fine, preferred_element_type=F32) + b_fine.astype(F32)).reshape(T, N_GROUPS, EXPERTS_PER_GROUP)
    lf = jnp.take_along_axis(lf, grp[:, None, None], axis=1)[:, 0]
    top_v, top_i = lax.top_k(lf, TOP_K)
    gate = jax.nn.softmax(top_v, axis=-1) * p_grp
    n_assign = T * TOP_K
    eid = (grp[:, None] * EXPERTS_PER_GROUP + top_i.astype(jnp.int32)).reshape(n_assign)
    wgt = gate.reshape(n_assign)
    tok = jnp.repeat(jnp.arange(T, dtype=jnp.int32), TOP_K)
    order = jnp.argsort(eid)
    e_s, tok_s, w_s = eid[order], tok[order], wgt[order]
    counts = jnp.bincount(eid, length=N_EXPERTS).astype(jnp.int32)
    start = jnp.cumsum(counts) - counts
    pcounts = (counts + MOE_BLOCK - 1) // MOE_BLOCK * MOE_BLOCK
    pend = jnp.cumsum(pcounts)
    pstart = pend - pcounts
    dest = pstart[e_s] + jnp.arange(n_assign, dtype=jnp.int32) - start[e_s]
    n_blocks = -(-n_assign // MOE_BLOCK) + N_EXPERTS
    n_slots = n_blocks * MOE_BLOCK
    slot_tok = jnp.full((n_slots,), T, jnp.int32).at[dest].set(tok_s)
    slot_w = jnp.zeros((n_slots,), F32).at[dest].set(w_s)
    block_e = jnp.clip(jnp.searchsorted(pend, jnp.arange(n_blocks, dtype=jnp.int32) * MOE_BLOCK, side='right'),
                       0, N_EXPERTS - 1)
    x_pad = jnp.concatenate([x2d, jnp.zeros((1, D), x2d.dtype)], axis=0)
    xb = x_pad[slot_tok].reshape(n_blocks, MOE_BLOCK, D)

    def expert_block(args):
        xblk, e = args
        hu = xblk @ w_up[e]
        return (jax.nn.silu(hu[:, :D_EXPERT]) * hu[:, D_EXPERT:]) @ w_down[e]

    yb = lax.map(expert_block, (xb, block_e)).reshape(n_slots, D)
    out = jnp.zeros((T + 1, D), yb.dtype).at[slot_tok].add(yb * slot_w[:, None].astype(yb.dtype))
    return out[:T]


def trunk_layer(x, mods, p, ctx, rope):
    B, L, _ = x.shape
    m = mods[:, :, None, :]
    h = x * (1.0 + m[:, 1]) + m[:, 0]
    z = h @ p['w_in']
    offs = np.cumsum(IN_SIZES)[:-1].tolist()
    rq, rk, rv, rg, su, sv, aq, ak, av = jnp.split(z, offs, axis=-1)

    def heads(t):
        return t.reshape(B, L, -1, HEAD_DIM)

    rq, rk, rv = heads(rq), heads(rk), heads(rv)
    if ctx is None:
        s0_f = jnp.zeros((B, H_RET, HEAD_DIM, HEAD_DIM), F32)
        s0_b = s0_f
    else:
        s0_f, s0_b = ctx[0][:, 0], ctx[0][:, 1]
    o_f, s_f = retention_scan(rq, rk, rv, p['ret_log_decay'][0], s0_f)
    o_b, s_b = retention_scan(rq[:, ::-1], rk[:, ::-1], rv[:, ::-1], p['ret_log_decay'][1], s0_b)
    ret = (_normalize(o_f + o_b[:, ::-1]).reshape(B, L, D_RET) * p['ret_gn_w']
           * jax.nn.silu(rg.astype(F32))).astype(x.dtype)

    sm = spatial_gating(jax.nn.gelu(su), jax.nn.gelu(sv), p['sm_ln_w'], p['sm_w'], p['sm_b'])

    q = rms_norm(heads(aq), p['q_norm_w'])
    k = rms_norm(heads(ak), p['k_norm_w'])
    v = heads(av)
    if ctx is None:
        new_ctx = (jnp.stack([s_f, s_b], axis=1), k, v)
        keys, vals = k, v
    else:
        cos, sin = rope
        q = apply_axial_rope(q, cos, sin)
        k = apply_axial_rope(k, cos, sin)
        keys = jnp.concatenate([k, ctx[1].astype(k.dtype)], axis=1)
        vals = jnp.concatenate([v, ctx[2].astype(v.dtype)], axis=1)
        new_ctx = None
    att = blocked_attention(q, keys, vals).astype(x.dtype)

    mix = jnp.concatenate([ret, sm.astype(x.dtype), att], axis=-1) @ p['w_out']
    x = layer_norm(DEEPNORM_ALPHA * x + m[:, 2] * mix, p['ln1_w'], p['ln1_b'])
    h2 = x * (1.0 + m[:, 4]) + m[:, 3]
    ffn = hier_moe(h2.reshape(B * L, D_MODEL), p['router_w_coarse'], p['router_b_coarse'],
                   p['router_w_fine'], p['router_b_fine'], p['w_up'], p['w_down']).reshape(B, L, D_MODEL)
    x = layer_norm(DEEPNORM_ALPHA * x + m[:, 5] * ffn, p['ln2_w'], p['ln2_b'])
    return x, new_ctx


def setup_inputs(seed: int = 0) -> dict:
    key = jax.random.key(seed)
    ks = iter(jax.random.split(key, 40))

    def nrm(shape, scale):
        return jax.random.normal(next(ks), shape, F32) * scale

    exps = 5.0 + jnp.arange(H_RET, dtype=F32)[None, None, :] + jax.random.uniform(next(ks), (DEPTH, 2, H_RET), F32)
    ret_log_decay = jnp.log1p(-jnp.exp2(-exps))
    return {
        'x_prompt': nrm((BATCH, SEQ, D_MODEL), 1.0),
        'x_sample': nrm((DEC_BATCH, DEC_SEQ, D_MODEL), 1.0),
        'state_ret': nrm((DEC_BATCH, DEPTH, 2, H_RET, HEAD_DIM, HEAD_DIM), 1.0),
        'cache_k': nrm((DEC_BATCH, DEPTH, PAST_LEN, KV_HEADS, HEAD_DIM), 1.0),
        'cache_v': nrm((DEC_BATCH, DEPTH, PAST_LEN, KV_HEADS, HEAD_DIM), 1.0),
        'c': nrm((DEC_BATCH, D_MODEL), 1.0),
        'c_ctx': nrm((D_MODEL,), 1.0),
        'w_mod': nrm((DEPTH, D_MODEL, N_MOD * D_MODEL), 0.5 * D_MODEL ** -0.5),
        'b_mod': nrm((DEPTH, N_MOD * D_MODEL), 0.02),
        'w_in': nrm((DEPTH, D_MODEL, N_IN), D_MODEL ** -0.5),
        'ret_log_decay': ret_log_decay,
        'ret_gn_w': 1.0 + nrm((DEPTH, D_RET), 0.02),
        'sm_ln_w': 1.0 + nrm((DEPTH, D_SM), 0.02),
        'sm_w': nrm((DEPTH, SM_GROUPS, CHUNK, CHUNK), CHUNK ** -0.5),
        'sm_b': 1.0 + nrm((DEPTH, SM_GROUPS, CHUNK), 0.02),
        'q_norm_w': 1.0 + nrm((DEPTH, HEAD_DIM), 0.02),
        'k_norm_w': 1.0 + nrm((DEPTH, HEAD_DIM), 0.02),
        'w_out': nrm((DEPTH, D_MIX, D_MODEL), DEEPNORM_BETA * D_MIX ** -0.5),
        'ln1_w': 1.0 + nrm((DEPTH, D_MODEL), 0.02),
        'ln1_b': nrm((DEPTH, D_MODEL), 0.02),
        'router_w_coarse': nrm((DEPTH, D_MODEL, N_GROUPS), D_MODEL ** -0.5),
        'router_b_coarse': nrm((DEPTH, N_GROUPS), 0.01),
        'router_w_fine': nrm((DEPTH, D_MODEL, N_EXPERTS), D_MODEL ** -0.5),
        'router_b_fine': nrm((DEPTH, N_EXPERTS), 0.01),
        'w_up': nrm((DEPTH, N_EXPERTS, D_MODEL, 2 * D_EXPERT), D_MODEL ** -0.5),
        'w_down': nrm((DEPTH, N_EXPERTS, D_EXPERT, D_MODEL), DEEPNORM_BETA * D_EXPERT ** -0.5),
        'ln2_w': 1.0 + nrm((DEPTH, D_MODEL), 0.02),
        'ln2_b': nrm((DEPTH, D_MODEL), 0.02),
    }


def reference(x_prompt, x_sample, state_ret, cache_k, cache_v, c, c_ctx, w_mod, b_mod, w_in,
              ret_log_decay, ret_gn_w, sm_ln_w, sm_w, sm_b, q_norm_w, k_norm_w, w_out, ln1_w, ln1_b,
              router_w_coarse, router_b_coarse, router_w_fine, router_b_fine, w_up, w_down, ln2_w, ln2_b):
    rope = axial_rope_tables(x_sample.shape[1])
    y_prompt, y_sample = x_prompt, x_sample
    ret_states, ctx_keys, ctx_vals = [], [], []
    for l in range(DEPTH):
        p = {
            'w_in': w_in[l], 'ret_log_decay': ret_log_decay[l], 'ret_gn_w': ret_gn_w[l],
            'sm_ln_w': sm_ln_w[l], 'sm_w': sm_w[l], 'sm_b': sm_b[l],
            'q_norm_w': q_norm_w[l], 'k_norm_w': k_norm_w[l], 'w_out': w_out[l],
            'ln1_w': ln1_w[l], 'ln1_b': ln1_b[l],
            'router_w_coarse': router_w_coarse[l], 'router_b_coarse': router_b_coarse[l],
            'router_w_fine': router_w_fine[l], 'router_b_fine': router_b_fine[l],
            'w_up': w_up[l], 'w_down': w_down[l], 'ln2_w': ln2_w[l], 'ln2_b': ln2_b[l],
        }
        mods_ctx = adaln_mods(c_ctx[None, :], w_mod[l], b_mod[l])
        y_prompt, (s_l, k_l, v_l) = trunk_layer(y_prompt, mods_ctx, p, None, None)
        ret_states.append(s_l)
        ctx_keys.append(k_l)
        ctx_vals.append(v_l)
        mods_lat = adaln_mods(c, w_mod[l], b_mod[l])
        y_sample, _ = trunk_layer(y_sample, mods_lat, p, (state_ret[:, l], cache_k[:, l], cache_v[:, l]), rope)
    new_state_ret = jnp.stack(ret_states, axis=1)
    new_cache_k = jnp.stack(ctx_keys, axis=1)
    new_cache_v = jnp.stack(ctx_vals, axis=1)
    return (y_prompt, y_sample, new_state_ret, new_cache_k, new_cache_v)
```

```python
import functools

import numpy as np
import jax
import jax.numpy as jnp
from jax import lax
from jax.experimental import pallas as pl
from jax.experimental.pallas import tpu as pltpu

F32 = jnp.float32
BF16 = jnp.bfloat16

D_MODEL = 4096
DEPTH = 2
GRID_W = 64
HEAD_DIM = 128
H_RET = 8
D_RET = H_RET * HEAD_DIM
SM_GROUPS = 8
D_SM = 1024
H_ATT = 16
KV_HEADS = 4
Q_PER_KV = H_ATT // KV_HEADS
D_ATT = H_ATT * HEAD_DIM
D_KV = KV_HEADS * HEAD_DIM
CHUNK = 128
ROPE_PAIRS = 32
ROPE_THETA = 10000.0
N_GROUPS = 4
EXPERTS_PER_GROUP = 8
N_EXPERTS = N_GROUPS * EXPERTS_PER_GROUP
D_EXPERT = D_MODEL // 4
N_MOD = 6
N_IN = D_RET * 4 + D_SM * 2 + D_ATT + D_KV * 2
DEEPNORM_ALPHA = (2 * DEPTH) ** 0.25
LN_EPS = 1e-5
RMS_EPS = 1e-6

COL_RQ, COL_RK, COL_RV, COL_RG = 0, H_RET, 2 * H_RET, 3 * H_RET
COL_SU, COL_SV = 4, 5
COL_AQ = 3
COL_AK, COL_AV = 16, 17
OFF_AV = COL_AV * D_KV

LANES = 128
ROUTER_PAD = LANES
MIB = 1024 * 1024
MOE_ROWS = 256
TM = 1024
TN = 1024
LN_ROWS = 128
MOD_ROWS = 512


def _cparams(sem, vmem_mib):
    return pltpu.CompilerParams(dimension_semantics=sem, vmem_limit_bytes=vmem_mib * MIB)


def _sigmoid(x):
    return 1.0 / (1.0 + jnp.exp(-x))


def _silu(x):
    return x * _sigmoid(x)


def _mods_kernel(c_ref, w_ref, b_ref, o_ref):
    a = _silu(c_ref[...]).astype(BF16)
    o_ref[...] = jnp.dot(a, w_ref[...].astype(BF16), preferred_element_type=F32) + b_ref[...]


def _adaln_mods(cond, w_mod, b_mod):
    n = cond.shape[0]
    tn = 512
    n_out = N_MOD * D_MODEL
    return pl.pallas_call(
        _mods_kernel,
        out_shape=jax.ShapeDtypeStruct((DEPTH, n, n_out), F32),
        grid=(DEPTH, n_out // tn),
        in_specs=[
            pl.BlockSpec((n, D_MODEL), lambda l, j: (0, 0)),
            pl.BlockSpec((None, D_MODEL, tn), lambda l, j: (l, 0, j)),
            pl.BlockSpec((None, 1, tn), lambda l, j: (l, 0, j)),
        ],
        out_specs=pl.BlockSpec((None, n, tn), lambda l, j: (l, 0, j)),
        compiler_params=_cparams(("arbitrary", "arbitrary"), 40),
        name="adaln_mods",
    )(cond, w_mod, b_mod.reshape(DEPTH, 1, n_out))


def _modulate_kernel(x_ref, m_ref, o_ref):
    o_ref[...] = (x_ref[...] * (1.0 + m_ref[1:2, :]) + m_ref[0:1, :]).astype(BF16)


def _mods_index(bm):
    if bm == 1:
        return lambda b, i: (0, 0, 0)
    return lambda b, i: (b, 0, 0)


def _modulate(x, mods):
    B, L, D = x.shape
    tl = min(MOD_ROWS, L)
    return pl.pallas_call(
        _modulate_kernel,
        out_shape=jax.ShapeDtypeStruct((B, L, D), BF16),
        grid=(B, L // tl),
        in_specs=[
            pl.BlockSpec((None, tl, D), lambda b, i: (b, i, 0)),
            pl.BlockSpec((None, N_MOD, D), _mods_index(mods.shape[0])),
        ],
        out_specs=pl.BlockSpec((None, tl, D), lambda b, i: (b, i, 0)),
        compiler_params=_cparams(("arbitrary", "arbitrary"), 40),
        name="modulate",
    )(x, mods)


def _mm_kernel(*refs, n_a):
    o_ref = refs[2 * n_a]
    acc = None
    for a_ref, w_ref in zip(refs[:n_a], refs[n_a:2 * n_a]):
        d = jnp.dot(a_ref[...], w_ref[...], preferred_element_type=F32)
        acc = d if acc is None else acc + d
    o_ref[...] = acc.astype(o_ref.dtype)


def _matmul(a_list, w, layer, k_sizes, out_dtype):
    T = a_list[0].shape[0]
    N = w.shape[2]
    tm = min(TM, T)
    in_specs = [pl.BlockSpec((tm, k), lambda i, j: (i, 0)) for k in k_sizes]
    row = 0
    for k in k_sizes:
        assert row % k == 0
        in_specs.append(pl.BlockSpec((None, k, TN), functools.partial(lambda i, j, rb: (layer, rb, j), rb=row // k)))
        row += k
    return pl.pallas_call(
        functools.partial(_mm_kernel, n_a=len(a_list)),
        out_shape=jax.ShapeDtypeStruct((T, N), out_dtype),
        grid=(T // tm, N // TN),
        in_specs=in_specs,
        out_specs=pl.BlockSpec((tm, TN), lambda i, j: (i, j)),
        compiler_params=_cparams(("arbitrary", "arbitrary"), 56),
        name="matmul",
    )(*a_list, *([w] * len(a_list)))


def _normalize_rows(x):
    mu = jnp.mean(x, axis=-1, keepdims=True)
    xc = x - mu
    var = jnp.mean(xc * xc, axis=-1, keepdims=True)
    return xc * lax.rsqrt(var + LN_EPS)


def _ret_kernel(ld_ref, q_ref, k_ref, v_ref, g_ref, s0_ref, gnw_ref, o_ref, sout_ref, oacc_ref, *,
                layer, n_chunks, has_ctx):
    h = pl.program_id(1)
    lg_f = ld_ref[layer * 2 * H_RET + h]
    lg_b = ld_ref[layer * 2 * H_RET + H_RET + h]
    C = CHUNK
    ii = lax.broadcasted_iota(jnp.int32, (C, C), 0).astype(F32)
    jj = lax.broadcasted_iota(jnp.int32, (C, C), 1).astype(F32)
    diff = ii - jj
    mask = (jnp.where(diff >= 0, jnp.exp(lg_f * jnp.maximum(diff, 0.0)), 0.0)
            + jnp.where(diff <= 0, jnp.exp(lg_b * jnp.maximum(-diff, 0.0)), 0.0))
    pos = lax.broadcasted_iota(jnp.int32, (C, 1), 0).astype(F32)
    qdec_f = jnp.exp(lg_f * (pos + 1.0))
    kdec_f = jnp.exp(lg_f * (C - 1.0 - pos))
    qdec_b = jnp.exp(lg_b * (C - pos))
    kdec_b = jnp.exp(lg_b * pos)
    cdec_f = jnp.exp(jnp.full((1, HEAD_DIM), lg_f * C, F32))
    cdec_b = jnp.exp(jnp.full((1, HEAD_DIM), lg_b * C, F32))
    k_scale = HEAD_DIM ** -0.5
    nt = (((1,), (1,)), ((), ()))

    def load_chunk(c):
        rows = pl.ds(pl.multiple_of(c * C, C), C)
        q = q_ref[rows, :].astype(BF16)
        kf = k_ref[rows, :] * k_scale
        v = v_ref[rows, :].astype(BF16)
        return rows, q, kf, v

    def state_update(s, kf, kdec, cdec, v):
        kd = (kf * kdec).astype(BF16)
        return s * cdec + jnp.dot(kd.T, v, preferred_element_type=F32)

    def fwd(c, s):
        rows, q, kf, v = load_chunk(c)
        qk = lax.dot_general(q, kf.astype(BF16), nt, preferred_element_type=F32)
        a = (qk * mask).astype(BF16)
        o = (jnp.dot(a, v, preferred_element_type=F32)
             + jnp.dot(q, s.astype(BF16), preferred_element_type=F32) * qdec_f)
        oacc_ref[rows, :] = o
        return state_update(s, kf, kdec_f, cdec_f, v)

    def bwd(t, s):
        c = n_chunks - 1 - t
        rows, q, kf, v = load_chunk(c)
        o = oacc_ref[rows, :] + jnp.dot(q, s.astype(BF16), preferred_element_type=F32) * qdec_b
        y = _normalize_rows(o) * gnw_ref[...] * _silu(g_ref[rows, :])
        o_ref[rows, :] = y.astype(o_ref.dtype)
        return state_update(s, kf, kdec_b, cdec_b, v)

    if has_ctx:
        s0_f = s0_ref[0]
        s0_b = s0_ref[1]
    else:
        s0_f = jnp.zeros((HEAD_DIM, HEAD_DIM), F32)
        s0_b = s0_f
    sout_ref[0] = lax.fori_loop(0, n_chunks, fwd, s0_f)
    sout_ref[1] = lax.fori_loop(0, n_chunks, bwd, s0_b)


def _retention(z3, log_decay_flat, state_ret, gn_w, layer, has_ctx):
    B, L, _ = z3.shape
    col = lambda off: pl.BlockSpec((None, L, HEAD_DIM), lambda b, h, ld: (b, 0, off + h))
    if has_ctx:
        s0 = state_ret
        s0_spec = pl.BlockSpec((None, None, 2, None, HEAD_DIM, HEAD_DIM), lambda b, h, ld: (b, layer, 0, h, 0, 0))
    else:
        s0 = jnp.zeros((1, 1, 2, 1, HEAD_DIM, HEAD_DIM), F32)
        s0_spec = pl.BlockSpec((None, None, 2, None, HEAD_DIM, HEAD_DIM), lambda b, h, ld: (0, 0, 0, 0, 0, 0))
    grid_spec = pltpu.PrefetchScalarGridSpec(
        num_scalar_prefetch=1,
        grid=(B, H_RET),
        in_specs=[col(COL_RQ), col(COL_RK), col(COL_RV), col(COL_RG), s0_spec,
                  pl.BlockSpec((None, 1, HEAD_DIM), lambda b, h, ld: (layer, 0, h))],
        out_specs=[pl.BlockSpec((None, L, HEAD_DIM), lambda b, h, ld: (b, 0, h)),
                   pl.BlockSpec((None, 2, None, HEAD_DIM, HEAD_DIM), lambda b, h, ld: (b, 0, h, 0, 0))],
        scratch_shapes=[pltpu.VMEM((L, HEAD_DIM), F32)],
    )
    return pl.pallas_call(
        functools.partial(_ret_kernel, layer=layer, n_chunks=L // CHUNK, has_ctx=has_ctx),
        out_shape=[jax.ShapeDtypeStruct((B, L, D_RET), BF16),
                   jax.ShapeDtypeStruct((B, 2, H_RET, HEAD_DIM, HEAD_DIM), F32)],
        grid_spec=grid_spec,
        compiler_params=_cparams(("arbitrary", "arbitrary"), 40),
        name="retention",
    )(log_decay_flat, z3, z3, z3, z3, s0, gn_w.reshape(DEPTH, 1, D_RET))


def _sg_kernel(u_ref, v_ref, lnw_ref, ws_ref, bs_ref, o_ref, *, n_sub):
    for s in range(n_sub):
        rows = slice(s * CHUNK, (s + 1) * CHUNK)
        u = jax.nn.gelu(u_ref[rows, :])
        vn = (_normalize_rows(jax.nn.gelu(v_ref[rows, :])) * lnw_ref[...]).astype(BF16)
        for g in range(SM_GROUPS):
            cols = slice(g * LANES, (g + 1) * LANES)
            sg = jnp.dot(ws_ref[g], vn[:, cols], preferred_element_type=F32) + bs_ref[:, g:g + 1]
            o_ref[rows, cols] = (u[:, cols] * sg).astype(o_ref.dtype)


def _spatial_gating(z3, sm_ln_w, sm_w16, sm_bT, layer):
    B, L, _ = z3.shape
    ts = 2 * CHUNK
    return pl.pallas_call(
        functools.partial(_sg_kernel, n_sub=ts // CHUNK),
        out_shape=jax.ShapeDtypeStruct((B, L, D_SM), BF16),
        grid=(B, L // ts),
        in_specs=[
            pl.BlockSpec((None, ts, D_SM), lambda b, i: (b, i, COL_SU)),
            pl.BlockSpec((None, ts, D_SM), lambda b, i: (b, i, COL_SV)),
            pl.BlockSpec((None, 1, D_SM), lambda b, i: (layer, 0, 0)),
            pl.BlockSpec((None, SM_GROUPS, CHUNK, CHUNK), lambda b, i: (layer, 0, 0, 0)),
            pl.BlockSpec((None, CHUNK, SM_GROUPS), lambda b, i: (layer, 0, 0)),
        ],
        out_specs=pl.BlockSpec((None, ts, D_SM), lambda b, i: (b, i, 0)),
        compiler_params=_cparams(("arbitrary", "arbitrary"), 40),
        name="spatial_gating",
    )(z3, z3, sm_ln_w.reshape(DEPTH, 1, D_SM), sm_w16, sm_bT)


def _rms_rows(x, w):
    return x * lax.rsqrt(jnp.mean(x * x, axis=-1, keepdims=True) + RMS_EPS) * w


def _qk_kernel(*refs, rope):
    if rope:
        q_ref, k_ref, qw_ref, kw_ref, cos_ref, sin_ref, qo_ref, ko_ref = refs
        cos = cos_ref[...]
        sin = sin_ref[...]
        lane = lax.broadcasted_iota(jnp.int32, cos.shape, 1)
        first_half = (lane % (2 * ROPE_PAIRS)) < ROPE_PAIRS
    else:
        q_ref, k_ref, qw_ref, kw_ref, qo_ref, ko_ref, kf_ref = refs

    def rot(x):
        if not rope:
            return x
        partner = jnp.where(first_half, pltpu.roll(x, LANES - ROPE_PAIRS, axis=1), pltpu.roll(x, ROPE_PAIRS, axis=1))
        return x * cos + partner * sin

    for hd in range(H_ATT):
        cols = slice(hd * HEAD_DIM, (hd + 1) * HEAD_DIM)
        qo_ref[:, cols] = rot(_rms_rows(q_ref[:, cols], qw_ref[...])).astype(qo_ref.dtype)
    for hd in range(KV_HEADS):
        cols = slice(hd * HEAD_DIM, (hd + 1) * HEAD_DIM)
        kn = _rms_rows(k_ref[:, cols], kw_ref[...])
        if not rope:
            kf_ref[:, cols] = kn
        ko_ref[:, cols] = rot(kn).astype(ko_ref.dtype)


def _qk_prep(z3, q_norm_w, k_norm_w, layer, rope_tabs):
    B, L, _ = z3.shape
    tl = 256
    rope = rope_tabs is not None
    in_specs = [
        pl.BlockSpec((None, tl, D_ATT), lambda b, i: (b, i, COL_AQ)),
        pl.BlockSpec((None, tl, D_KV), lambda b, i: (b, i, COL_AK)),
        pl.BlockSpec((None, 1, HEAD_DIM), lambda b, i: (layer, 0, 0)),
        pl.BlockSpec((None, 1, HEAD_DIM), lambda b, i: (layer, 0, 0)),
    ]
    args = [z3, z3, q_norm_w.reshape(DEPTH, 1, HEAD_DIM), k_norm_w.reshape(DEPTH, 1, HEAD_DIM)]
    out_shape = [jax.ShapeDtypeStruct((B, L, D_ATT), BF16), jax.ShapeDtypeStruct((B, L, D_KV), BF16)]
    out_specs = [pl.BlockSpec((None, tl, D_ATT), lambda b, i: (b, i, 0)),
                 pl.BlockSpec((None, tl, D_KV), lambda b, i: (b, i, 0))]
    if rope:
        in_specs += [pl.BlockSpec((tl, HEAD_DIM), lambda b, i: (i, 0))] * 2
        args += list(rope_tabs)
    else:
        out_shape.append(jax.ShapeDtypeStruct((B, L, D_KV), F32))
        out_specs.append(pl.BlockSpec((None, tl, D_KV), lambda b, i: (b, i, 0)))
    return pl.pallas_call(
        functools.partial(_qk_kernel, rope=rope),
        out_shape=out_shape,
        grid=(B, L // tl),
        in_specs=in_specs,
        out_specs=out_specs,
        compiler_params=_cparams(("arbitrary", "arbitrary"), 40),
        name="qk_prep",
    )(*args)


def _rope_tables(n_tokens):
    rows = n_tokens // GRID_W
    row = jnp.broadcast_to(jnp.arange(rows, dtype=F32)[:, None], (rows, GRID_W)).reshape(-1)
    col = jnp.broadcast_to(jnp.arange(GRID_W, dtype=F32)[None, :], (rows, GRID_W)).reshape(-1)
    inv = ROPE_THETA ** (-2.0 * jnp.arange(ROPE_PAIRS, dtype=F32) / (2 * ROPE_PAIRS))
    ang_r = row[:, None] * inv
    ang_c = col[:, None] * inv
    cos = jnp.concatenate([jnp.cos(ang_r), jnp.cos(ang_r), jnp.cos(ang_c), jnp.cos(ang_c)], axis=1)
    sin = jnp.concatenate([-jnp.sin(ang_r), jnp.sin(ang_r), -jnp.sin(ang_c), jnp.sin(ang_c)], axis=1)
    return cos, sin


def _att_kernel(q_ref, k_ref, v_ref, o_ref):
    k = k_ref[...]
    v = v_ref[...]
    scale = HEAD_DIM ** -0.5
    nt = (((1,), (1,)), ((), ()))
    for g in range(Q_PER_KV):
        cols = slice(g * HEAD_DIM, (g + 1) * HEAD_DIM)
        s = lax.dot_general(q_ref[:, cols], k, nt, preferred_element_type=F32) * scale
        p = jnp.exp(s - jnp.max(s, axis=-1, keepdims=True))
        l = jnp.sum(p, axis=-1, keepdims=True)
        o = jnp.dot(p.astype(BF16), v, preferred_element_type=F32) / l
        o_ref[:, cols] = o.astype(o_ref.dtype)


def _attention(q, keys, vals):
    B, L, _ = q.shape
    Lk = keys.shape[1]
    tq = 256
    blk = Q_PER_KV * HEAD_DIM
    return pl.pallas_call(
        _att_kernel,
        out_shape=jax.ShapeDtypeStruct((B, L, D_ATT), BF16),
        grid=(B, KV_HEADS, L // tq),
        in_specs=[
            pl.BlockSpec((None, tq, blk), lambda b, h, i: (b, i, h)),
            pl.BlockSpec((None, Lk, HEAD_DIM), lambda b, h, i: (b, 0, h)),
            pl.BlockSpec((None, Lk, HEAD_DIM), lambda b, h, i: (b, 0, h)),
        ],
        out_specs=pl.BlockSpec((None, tq, blk), lambda b, h, i: (b, i, h)),
        compiler_params=_cparams(("arbitrary", "arbitrary", "arbitrary"), 48),
        name="attention",
    )(q, keys, vals)


def _ln_kernel(*refs, n_y, gate_idx, mod_idx, with_router):
    x_ref = refs[0]
    y_refs = refs[1:1 + n_y]
    mg_ref, lnw_ref, lnb_ref = refs[1 + n_y:4 + n_y]
    rest = list(refs[4 + n_y:])
    y = y_refs[0][...]
    for r in y_refs[1:]:
        y = y + r[...]
    t = DEEPNORM_ALPHA * x_ref[...] + mg_ref[gate_idx:gate_idx + 1, :] * y
    xn = _normalize_rows(t) * lnw_ref[...] + lnb_ref[...]
    if mod_idx is None:
        (xo_ref,) = rest
        xo_ref[...] = xn
        return
    mh_ref = rest.pop(0)
    h = xn * (1.0 + mh_ref[mod_idx + 1:mod_idx + 2, :]) + mh_ref[mod_idx:mod_idx + 1, :]
    if with_router:
        wr_ref, br_ref, xo_ref, ho_ref, lg_ref = rest
        lg_ref[...] = jnp.dot(h, wr_ref[...], preferred_element_type=F32,
                              precision=lax.Precision.HIGHEST) + br_ref[...]
    else:
        xo_ref, ho_ref = rest
    xo_ref[...] = xn
    ho_ref[...] = h.astype(ho_ref.dtype)


def _ln_block(x, ys, mods_gate, gate_idx, ln_w, ln_b, layer, mods_h=None, mod_idx=None, router=None):
    B, L, D = x.shape
    tl = LN_ROWS
    row_spec = pl.BlockSpec((None, tl, D), lambda b, i: (b, i, 0))
    in_specs = [row_spec]
    args = [x]
    for arr, lead in ys:
        in_specs.append(pl.BlockSpec((None,) * len(lead) + (None, tl, D),
                                     functools.partial(lambda b, i, lead: lead + (b, i, 0), lead=lead)))
        args.append(arr)
    vec_spec = pl.BlockSpec((None, 1, D), lambda b, i: (layer, 0, 0))
    in_specs += [pl.BlockSpec((None, N_MOD, D), _mods_index(mods_gate.shape[0])), vec_spec, vec_spec]
    args += [mods_gate, ln_w.reshape(DEPTH, 1, D), ln_b.reshape(DEPTH, 1, D)]
    out_shape = [jax.ShapeDtypeStruct((B, L, D), F32)]
    out_specs = [row_spec]
    if mod_idx is not None:
        in_specs.append(pl.BlockSpec((None, N_MOD, D), _mods_index(mods_h.shape[0])))
        args.append(mods_h)
        if router is not None:
            w_r, b_r = router
            in_specs += [pl.BlockSpec((None, D, ROUTER_PAD), lambda b, i: (layer, 0, 0)),
                         pl.BlockSpec((None, 1, ROUTER_PAD), lambda b, i: (layer, 0, 0))]
            args += [w_r, b_r]
            out_shape += [jax.ShapeDtypeStruct((B, L, D), F32), jax.ShapeDtypeStruct((B, L, ROUTER_PAD), F32)]
            out_specs += [row_spec, pl.BlockSpec((None, tl, ROUTER_PAD), lambda b, i: (b, i, 0))]
        else:
            out_shape.append(jax.ShapeDtypeStruct((B, L, D), BF16))
            out_specs.append(row_spec)
    return pl.pallas_call(
        functools.partial(_ln_kernel, n_y=len(ys), gate_idx=gate_idx, mod_idx=mod_idx,
                          with_router=router is not None),
        out_shape=out_shape,
        grid=(B, L // tl),
        in_specs=in_specs,
        out_specs=out_specs,
        compiler_params=_cparams(("arbitrary", "arbitrary"), 48),
        name="layernorm",
    )(*args)


def _gather_kernel(idx_ref, src_ref, o_ref, buf_ref, sem_ref, *, rows):
    b = pl.program_id(0)
    nb = pl.num_programs(0)

    def issue(block, slot):
        base = block * rows

        def body(r, carry):
            t = idx_ref[base + r]
            pltpu.make_async_copy(src_ref.at[pl.ds(t, 1), :], buf_ref.at[slot, pl.ds(r, 1), :],
                                  sem_ref.at[slot]).start()
            return carry

        lax.fori_loop(0, rows, body, 0)

    slot = b % 2

    @pl.when(b == 0)
    def _():
        issue(0, 0)

    @pl.when(b + 1 < nb)
    def _():
        issue(b + 1, 1 - slot)

    pltpu.make_async_copy(src_ref.at[pl.ds(0, rows), :], buf_ref.at[slot], sem_ref.at[slot]).wait()
    o_ref[...] = buf_ref[slot].astype(o_ref.dtype)


def _gather_rows(src, idx, out_dtype):
    M = idx.shape[0]
    D = src.shape[1]
    rows = MOE_ROWS
    grid_spec = pltpu.PrefetchScalarGridSpec(
        num_scalar_prefetch=1,
        grid=(M // rows,),
        in_specs=[pl.BlockSpec(memory_space=pl.ANY)],
        out_specs=pl.BlockSpec((rows, D), lambda b, idx: (b, 0)),
        scratch_shapes=[pltpu.VMEM((2, rows, D), src.dtype), pltpu.SemaphoreType.DMA((2,))],
    )
    return pl.pallas_call(
        functools.partial(_gather_kernel, rows=rows),
        out_shape=jax.ShapeDtypeStruct((M, D), out_dtype),
        grid_spec=grid_spec,
        compiler_params=_cparams(("arbitrary",), 40),
        name="gather_rows",
    )(idx, src)


def _up_kernel(be_ref, nu_ref, x_ref, wg_ref, wu_ref, o_ref):
    b = pl.program_id(0)

    @pl.when(b < nu_ref[0])
    def _():
        x = x_ref[...]
        hg = jnp.dot(x, wg_ref[...], preferred_element_type=F32)
        hu = jnp.dot(x, wu_ref[...], preferred_element_type=F32)
        o_ref[...] = (_silu(hg) * hu).astype(o_ref.dtype)

    @pl.when(b >= nu_ref[0])
    def _():
        o_ref[...] = jnp.zeros_like(o_ref)


def _down_kernel(be_ref, nu_ref, a_ref, wd_ref, sw_ref, o_ref):
    b = pl.program_id(0)

    @pl.when(b < nu_ref[0])
    def _():
        o_ref[...] = jnp.dot(a_ref[...], wd_ref[...], preferred_element_type=F32) * sw_ref[...]

    @pl.when(b >= nu_ref[0])
    def _():
        o_ref[...] = jnp.zeros_like(o_ref)


def _experts(xb, block_e, n_used, slot_w, w_up16, w_down16, layer):
    n_slots = xb.shape[0]
    nb = n_slots // MOE_ROWS
    up_spec = pltpu.PrefetchScalarGridSpec(
        num_scalar_prefetch=2,
        grid=(nb,),
        in_specs=[
            pl.BlockSpec((MOE_ROWS, D_MODEL), lambda b, be, nu: (b, 0)),
            pl.BlockSpec((None, None, D_MODEL, D_EXPERT), lambda b, be, nu: (layer, be[b], 0, 0)),
            pl.BlockSpec((None, None, D_MODEL, D_EXPERT), lambda b, be, nu: (layer, be[b], 0, 1)),
        ],
        out_specs=pl.BlockSpec((MOE_ROWS, D_EXPERT), lambda b, be, nu: (b, 0)),
    )
    act = pl.pallas_call(
        _up_kernel,
        out_shape=jax.ShapeDtypeStruct((n_slots, D_EXPERT), BF16),
        grid_spec=up_spec,
        compiler_params=_cparams(("arbitrary",), 52),
        name="expert_up",
    )(block_e, n_used, xb, w_up16, w_up16)
    down_spec = pltpu.PrefetchScalarGridSpec(
        num_scalar_prefetch=2,
        grid=(nb,),
        in_specs=[
            pl.BlockSpec((MOE_ROWS, D_EXPERT), lambda b, be, nu: (b, 0)),
            pl.BlockSpec((None, None, D_EXPERT, D_MODEL), lambda b, be, nu: (layer, be[b], 0, 0)),
            pl.BlockSpec((MOE_ROWS, 1), lambda b, be, nu: (b, 0)),
        ],
        out_specs=pl.BlockSpec((MOE_ROWS, D_MODEL), lambda b, be, nu: (b, 0)),
    )
    return pl.pallas_call(
        _down_kernel,
        out_shape=jax.ShapeDtypeStruct((n_slots, D_MODEL), F32),
        grid_spec=down_spec,
        compiler_params=_cparams(("arbitrary",), 48),
        name="expert_down",
    )(block_e, n_used, act, w_down16, slot_w.reshape(n_slots, 1))


def _route(logits):
    T = logits.shape[0]
    lc = logits[:, :N_GROUPS]
    lf = logits[:, N_GROUPS:N_GROUPS + N_EXPERTS].reshape(T, N_GROUPS, EXPERTS_PER_GROUP)
    grp = jnp.argmax(lc, axis=-1).astype(jnp.int32)
    p_grp = jnp.take_along_axis(jax.nn.softmax(lc, axis=-1), grp[:, None], axis=-1)
    lfg = jnp.take_along_axis(lf, grp[:, None, None], axis=1)[:, 0]
    top_v, top_i = lax.top_k(lfg, 2)
    gate = jax.nn.softmax(top_v, axis=-1) * p_grp
    eid = (grp[:, None] * EXPERTS_PER_GROUP + top_i.astype(jnp.int32)).T.reshape(-1)
    wgt = gate.T.reshape(-1)
    tok = jnp.tile(jnp.arange(T, dtype=jnp.int32), 2)
    n_assign = 2 * T
    counts = jnp.bincount(eid, length=N_EXPERTS).astype(jnp.int32)
    start = jnp.cumsum(counts) - counts
    pcounts = (counts + MOE_ROWS - 1) // MOE_ROWS * MOE_ROWS
    pend = jnp.cumsum(pcounts)
    pstart = pend - pcounts
    order = jnp.argsort(eid)
    e_s = eid[order]
    dest_s = pstart[e_s] + jnp.arange(n_assign, dtype=jnp.int32) - start[e_s]
    dest = jnp.zeros((n_assign,), jnp.int32).at[order].set(dest_s)
    n_blocks = n_assign // MOE_ROWS + N_EXPERTS
    n_slots = n_blocks * MOE_ROWS
    slot_tok = jnp.zeros((n_slots,), jnp.int32).at[dest].set(tok)
    slot_w = jnp.zeros((n_slots,), F32).at[dest].set(wgt)
    block_e = jnp.clip(jnp.searchsorted(pend, jnp.arange(n_blocks, dtype=jnp.int32) * MOE_ROWS, side='right'),
                       0, N_EXPERTS - 1).astype(jnp.int32)
    n_used = (pend[-1:] // MOE_ROWS).astype(jnp.int32)
    return slot_tok, slot_w, block_e, n_used, dest


def _layer(x, h, mods, mods_next, W, layer, ctx, rope_tabs):
    B, L, D = x.shape
    T = B * L
    z = _matmul([h.reshape(T, D)], W['w_in'], layer, (D,), F32)
    z3 = z.reshape(B, L, N_IN)

    ret, s_new = _retention(z3, W['log_decay'], ctx[0] if ctx else None, W['ret_gn_w'], layer, ctx is not None)
    sm = _spatial_gating(z3, W['sm_ln_w'], W['sm_w'], W['sm_bT'], layer)
    v_new = z3[:, :, OFF_AV:OFF_AV + D_KV]
    if ctx is None:
        q16, k16, k_new = _qk_prep(z3, W['q_norm_w'], W['k_norm_w'], layer, None)
        keys, vals = k16, v_new.astype(BF16)
        new_ctx = (s_new, k_new.reshape(B, L, KV_HEADS, HEAD_DIM), v_new.reshape(B, L, KV_HEADS, HEAD_DIM))
    else:
        q16, k16 = _qk_prep(z3, W['q_norm_w'], W['k_norm_w'], layer, rope_tabs)
        keys = jnp.concatenate([k16, ctx[1]], axis=1)
        vals = jnp.concatenate([v_new.astype(BF16), ctx[2]], axis=1)
        new_ctx = None
    att = _attention(q16, keys, vals)

    mix = _matmul([ret.reshape(T, D_RET), sm.reshape(T, D_SM), att.reshape(T, D_ATT)], W['w_out'], layer,
                  (D_RET, D_SM, D_ATT), F32)
    x1, h2, logits = _ln_block(x, [(mix.reshape(B, L, D), ())], mods, 2, W['ln1_w'], W['ln1_b'], layer,
                               mods_h=mods, mod_idx=3, router=(W['router_w'], W['router_b']))

    slot_tok, slot_w, block_e, n_used, dest = _route(logits.reshape(T, ROUTER_PAD))
    xb = _gather_rows(h2.reshape(T, D), slot_tok, BF16)
    yb = _experts(xb, block_e, n_used, slot_w, W['w_up'], W['w_down'], layer)
    ffn = _gather_rows(yb, dest, F32).reshape(2, B, L, D)
    ys = [(ffn, (0,)), (ffn, (1,))]
    if mods_next is None:
        (x2,) = _ln_block(x1, ys, mods, 5, W['ln2_w'], W['ln2_b'], layer)
        h_next = None
    else:
        x2, h_next = _ln_block(x1, ys, mods, 5, W['ln2_w'], W['ln2_b'], layer, mods_h=mods_next, mod_idx=0)
    return x2, h_next, new_ctx


def kernel(x_prompt, x_sample, state_ret, cache_k, cache_v, c, c_ctx, w_mod, b_mod, w_in, ret_log_decay, ret_gn_w, sm_ln_w, sm_w, sm_b, q_norm_w, k_norm_w, w_out, ln1_w, ln1_b, router_w_coarse, router_b_coarse, router_w_fine, router_b_fine, w_up, w_down, ln2_w, ln2_b):
    n_dec = c.shape[0]
    cond = jnp.zeros((16, D_MODEL), F32).at[:n_dec].set(c).at[n_dec].set(c_ctx)
    mods_all = _adaln_mods(cond, w_mod, b_mod)
    mods_lat = [mods_all[l, :n_dec].reshape(n_dec, N_MOD, D_MODEL) for l in range(DEPTH)]
    mods_ctx = [mods_all[l, n_dec:n_dec + 1].reshape(1, N_MOD, D_MODEL) for l in range(DEPTH)]

    n_r = N_GROUPS + N_EXPERTS
    router_w = jnp.zeros((DEPTH, D_MODEL, ROUTER_PAD), F32).at[:, :, :n_r].set(
        jnp.concatenate([router_w_coarse, router_w_fine], axis=-1))
    router_b = jnp.zeros((DEPTH, 1, ROUTER_PAD), F32).at[:, 0, :n_r].set(
        jnp.concatenate([router_b_coarse, router_b_fine], axis=-1))
    W = {
        'w_in': w_in.astype(BF16), 'w_out': w_out.astype(BF16),
        'w_up': w_up.astype(BF16), 'w_down': w_down.astype(BF16),
        'log_decay': ret_log_decay.reshape(-1), 'ret_gn_w': ret_gn_w,
        'sm_ln_w': sm_ln_w, 'sm_w': sm_w.astype(BF16), 'sm_bT': jnp.swapaxes(sm_b, 1, 2),
        'q_norm_w': q_norm_w, 'k_norm_w': k_norm_w,
        'ln1_w': ln1_w, 'ln1_b': ln1_b, 'ln2_w': ln2_w, 'ln2_b': ln2_b,
        'router_w': router_w, 'router_b': router_b,
    }
    rope_tabs = _rope_tables(x_sample.shape[1])
    past = cache_k.shape[2]
    ck16 = cache_k.astype(BF16).reshape(n_dec, DEPTH, past, D_KV)
    cv16 = cache_v.astype(BF16).reshape(n_dec, DEPTH, past, D_KV)

    y_p, y_s = x_prompt, x_sample
    h_p = _modulate(x_prompt, mods_ctx[0])
    h_s = _modulate(x_sample, mods_lat[0])
    states, ctx_k, ctx_v = [], [], []
    for l in range(DEPTH):
        nxt = l + 1 < DEPTH
        y_p, h_p, (s_l, k_l, v_l) = _layer(y_p, h_p, mods_ctx[l], mods_ctx[l + 1] if nxt else None, W, l, None, None)
        states.append(s_l)
        ctx_k.append(k_l)
        ctx_v.append(v_l)
        y_s, h_s, _ = _layer(y_s, h_s, mods_lat[l], mods_lat[l + 1] if nxt else None, W, l,
                             (state_ret, ck16[:, l], cv16[:, l]), rope_tabs)
    return (y_p, y_s, jnp.stack(states, axis=1), jnp.stack(ctx_k, axis=1), jnp.stack(ctx_v, axis=1))
```

```python
import functools

import numpy as np
import jax
import jax.numpy as jnp
from jax import lax
from jax.experimental import pallas as pl
from jax.experimental.pallas import tpu as pltpu

F32 = jnp.float32
BF16 = jnp.bfloat16

D_MODEL = 4096
DEPTH = 2
GRID_W = 64
HEAD_DIM = 128
H_RET = 8
D_RET = H_RET * HEAD_DIM
SM_GROUPS = 8
D_SM = 1024
H_ATT = 16
KV_HEADS = 4
Q_PER_KV = H_ATT // KV_HEADS
D_ATT = H_ATT * HEAD_DIM
D_KV = KV_HEADS * HEAD_DIM
CHUNK = 128
ROPE_PAIRS = 32
ROPE_THETA = 10000.0
N_GROUPS = 4
EXPERTS_PER_GROUP = 8
N_EXPERTS = N_GROUPS * EXPERTS_PER_GROUP
D_EXPERT = D_MODEL // 4
N_MOD = 6
N_IN = D_RET * 4 + D_SM * 2 + D_ATT + D_KV * 2
DEEPNORM_ALPHA = (2 * DEPTH) ** 0.25
LN_EPS = 1e-5
RMS_EPS = 1e-6
Q_PRESCALE = float(HEAD_DIM ** -0.5 * np.log2(np.e))

COL_RQ, COL_RK, COL_RV, COL_RG = 0, H_RET, 2 * H_RET, 3 * H_RET
COL_SU, COL_SV = 4, 5
COL_AQ = 3
COL_AK, COL_AV = 16, 17
OFF_AV = COL_AV * D_KV

LANES = 128
ROUTER_PAD = LANES
MIB = 1024 * 1024
MOE_ROWS = 256
TM = 1024
TN = 1024
LN_ROWS = 128
MOD_ROWS = 512


def _cparams(sem, vmem_mib):
    return pltpu.CompilerParams(dimension_semantics=sem, vmem_limit_bytes=vmem_mib * MIB)


def _sigmoid(x):
    return 1.0 / (1.0 + jnp.exp(-x))


def _silu(x):
    return x * _sigmoid(x)


def _mods_kernel(c_ref, w_ref, b_ref, o_ref):
    a = _silu(c_ref[...]).astype(BF16)
    o_ref[...] = jnp.dot(a, w_ref[...].astype(BF16), preferred_element_type=F32) + b_ref[...]


def _adaln_mods(cond, w_mod, b_mod):
    n = cond.shape[0]
    tn = 512
    n_out = N_MOD * D_MODEL
    return pl.pallas_call(
        _mods_kernel,
        out_shape=jax.ShapeDtypeStruct((DEPTH, n, n_out), F32),
        grid=(DEPTH, n_out // tn),
        in_specs=[
            pl.BlockSpec((n, D_MODEL), lambda l, j: (0, 0)),
            pl.BlockSpec((None, D_MODEL, tn), lambda l, j: (l, 0, j)),
            pl.BlockSpec((None, 1, tn), lambda l, j: (l, 0, j)),
        ],
        out_specs=pl.BlockSpec((None, n, tn), lambda l, j: (l, 0, j)),
        compiler_params=_cparams(("arbitrary", "arbitrary"), 40),
        name="adaln_mods",
    )(cond, w_mod, b_mod.reshape(DEPTH, 1, n_out))


def _modulate_kernel(x_ref, m_ref, o_ref):
    o_ref[...] = (x_ref[...] * (1.0 + m_ref[1:2, :]) + m_ref[0:1, :]).astype(BF16)


def _mods_index(bm):
    if bm == 1:
        return lambda b, i: (0, 0, 0)
    return lambda b, i: (b, 0, 0)


def _modulate(x, mods):
    B, L, D = x.shape
    tl = min(MOD_ROWS, L)
    return pl.pallas_call(
        _modulate_kernel,
        out_shape=jax.ShapeDtypeStruct((B, L, D), BF16),
        grid=(B, L // tl),
        in_specs=[
            pl.BlockSpec((None, tl, D), lambda b, i: (b, i, 0)),
            pl.BlockSpec((None, N_MOD, D), _mods_index(mods.shape[0])),
        ],
        out_specs=pl.BlockSpec((None, tl, D), lambda b, i: (b, i, 0)),
        compiler_params=_cparams(("arbitrary", "arbitrary"), 40),
        name="modulate",
    )(x, mods)


def _mm_kernel(*refs, n_a):
    o_ref = refs[2 * n_a]
    acc = None
    for a_ref, w_ref in zip(refs[:n_a], refs[n_a:2 * n_a]):
        d = jnp.dot(a_ref[...], w_ref[...], preferred_element_type=F32)
        acc = d if acc is None else acc + d
    o_ref[...] = acc.astype(o_ref.dtype)


def _matmul(a_list, w, layer, k_sizes, out_dtype):
    T = a_list[0].shape[0]
    N = w.shape[2]
    tm = min(TM, T)
    in_specs = [pl.BlockSpec((tm, k), lambda i, j: (i, 0)) for k in k_sizes]
    row = 0
    for k in k_sizes:
        assert row % k == 0
        in_specs.append(pl.BlockSpec((None, k, TN), functools.partial(lambda i, j, rb: (layer, rb, j), rb=row // k)))
        row += k
    return pl.pallas_call(
        functools.partial(_mm_kernel, n_a=len(a_list)),
        out_shape=jax.ShapeDtypeStruct((T, N), out_dtype),
        grid=(T // tm, N // TN),
        in_specs=in_specs,
        out_specs=pl.BlockSpec((tm, TN), lambda i, j: (i, j)),
        compiler_params=_cparams(("arbitrary", "arbitrary"), 56),
        name="matmul",
    )(*a_list, *([w] * len(a_list)))


def _normalize_rows(x):
    mu = jnp.mean(x, axis=-1, keepdims=True)
    xc = x - mu
    var = jnp.mean(xc * xc, axis=-1, keepdims=True)
    return xc * lax.rsqrt(var + LN_EPS)


RET_HEADS = 2


def _ret_consts(lg_f, lg_b):
    C = CHUNK
    ii = lax.broadcasted_iota(jnp.int32, (C, C), 0).astype(F32)
    jj = lax.broadcasted_iota(jnp.int32, (C, C), 1).astype(F32)
    diff = ii - jj
    mask = (jnp.where(diff >= 0, jnp.exp(lg_f * jnp.maximum(diff, 0.0)), 0.0)
            + jnp.where(diff <= 0, jnp.exp(lg_b * jnp.maximum(-diff, 0.0)), 0.0))
    pos = lax.broadcasted_iota(jnp.int32, (C, 1), 0).astype(F32)
    return dict(
        mask=mask,
        qdec_f=jnp.exp(lg_f * (pos + 1.0)), kdec_f=jnp.exp(lg_f * (C - 1.0 - pos)),
        qdec_b=jnp.exp(lg_b * (C - pos)), kdec_b=jnp.exp(lg_b * pos),
        cdec_f=jnp.exp(jnp.full((1, HEAD_DIM), lg_f * C, F32)),
        cdec_b=jnp.exp(jnp.full((1, HEAD_DIM), lg_b * C, F32)))


def _ret_kernel(ld_ref, q_ref, k_ref, v_ref, g_ref, s0_ref, gnw_ref, o_ref, sout_ref, oacc_ref, *,
                layer, n_chunks, has_ctx):
    hb = pl.program_id(1)
    C = CHUNK
    consts = []
    for j in range(RET_HEADS):
        h = hb * RET_HEADS + j
        consts.append(_ret_consts(ld_ref[layer * 2 * H_RET + h], ld_ref[layer * 2 * H_RET + H_RET + h]))
    k_scale = HEAD_DIM ** -0.5
    nt = (((1,), (1,)), ((), ()))

    def load_chunk(c, j):
        rows = pl.ds(pl.multiple_of(c * C, C), C)
        cols = slice(j * HEAD_DIM, (j + 1) * HEAD_DIM)
        q = q_ref[rows, cols].astype(BF16)
        kf = k_ref[rows, cols] * k_scale
        v = v_ref[rows, cols].astype(BF16)
        return rows, cols, q, kf, v

    def state_update(s, kf, kdec, cdec, v):
        kd = (kf * kdec).astype(BF16)
        return s * cdec + jnp.dot(kd.T, v, preferred_element_type=F32)

    def fwd(c, states):
        out = []
        for j, s in enumerate(states):
            cj = consts[j]
            rows, cols, q, kf, v = load_chunk(c, j)
            qk = lax.dot_general(q, kf.astype(BF16), nt, preferred_element_type=F32)
            a = (qk * cj['mask']).astype(BF16)
            o = (jnp.dot(a, v, preferred_element_type=F32)
                 + jnp.dot(q, s.astype(BF16), preferred_element_type=F32) * cj['qdec_f'])
            oacc_ref[rows, cols] = o
            out.append(state_update(s, kf, cj['kdec_f'], cj['cdec_f'], v))
        return tuple(out)

    def bwd(t, states):
        c = n_chunks - 1 - t
        out = []
        for j, s in enumerate(states):
            cj = consts[j]
            rows, cols, q, kf, v = load_chunk(c, j)
            o = oacc_ref[rows, cols] + jnp.dot(q, s.astype(BF16), preferred_element_type=F32) * cj['qdec_b']
            y = _normalize_rows(o) * gnw_ref[:, cols] * _silu(g_ref[rows, cols])
            o_ref[rows, cols] = y.astype(o_ref.dtype)
            out.append(state_update(s, kf, cj['kdec_b'], cj['cdec_b'], v))
        return tuple(out)

    if has_ctx:
        s0_f = tuple(s0_ref[0, j] for j in range(RET_HEADS))
        s0_b = tuple(s0_ref[1, j] for j in range(RET_HEADS))
    else:
        s0_f = tuple(jnp.zeros((HEAD_DIM, HEAD_DIM), F32) for _ in range(RET_HEADS))
        s0_b = s0_f
    s_f = lax.fori_loop(0, n_chunks, fwd, s0_f, unroll=2)
    s_b = lax.fori_loop(0, n_chunks, bwd, s0_b, unroll=2)
    for j in range(RET_HEADS):
        sout_ref[0, j] = s_f[j]
        sout_ref[1, j] = s_b[j]


def _retention(z3, log_decay_flat, state_ret, gn_w, layer, has_ctx):
    B, L, _ = z3.shape
    hp = RET_HEADS
    width = hp * HEAD_DIM
    col = lambda off: pl.BlockSpec((None, L, width), lambda b, h, ld: (b, 0, off // hp + h))
    state_block = (None, None, 2, hp, HEAD_DIM, HEAD_DIM)
    if has_ctx:
        s0 = state_ret
        s0_spec = pl.BlockSpec(state_block, lambda b, h, ld: (b, layer, 0, h, 0, 0))
    else:
        s0 = jnp.zeros((1, 1, 2, hp, HEAD_DIM, HEAD_DIM), F32)
        s0_spec = pl.BlockSpec(state_block, lambda b, h, ld: (0, 0, 0, 0, 0, 0))
    grid_spec = pltpu.PrefetchScalarGridSpec(
        num_scalar_prefetch=1,
        grid=(B, H_RET // hp),
        in_specs=[col(COL_RQ), col(COL_RK), col(COL_RV), col(COL_RG), s0_spec,
                  pl.BlockSpec((None, 1, width), lambda b, h, ld: (layer, 0, h))],
        out_specs=[pl.BlockSpec((None, L, width), lambda b, h, ld: (b, 0, h)),
                   pl.BlockSpec((None, 2, hp, HEAD_DIM, HEAD_DIM), lambda b, h, ld: (b, 0, h, 0, 0))],
        scratch_shapes=[pltpu.VMEM((L, width), F32)],
    )
    return pl.pallas_call(
        functools.partial(_ret_kernel, layer=layer, n_chunks=L // CHUNK, has_ctx=has_ctx),
        out_shape=[jax.ShapeDtypeStruct((B, L, D_RET), BF16),
                   jax.ShapeDtypeStruct((B, 2, H_RET, HEAD_DIM, HEAD_DIM), F32)],
        grid_spec=grid_spec,
        compiler_params=_cparams(("arbitrary", "arbitrary"), 52),
        name="retention",
    )(log_decay_flat, z3, z3, z3, z3, s0, gn_w.reshape(DEPTH, 1, D_RET))


def _sg_kernel(u_ref, v_ref, lnw_ref, ws_ref, bs_ref, o_ref, *, n_sub):
    for s in range(n_sub):
        rows = slice(s * CHUNK, (s + 1) * CHUNK)
        u = jax.nn.gelu(u_ref[rows, :])
        vn = (_normalize_rows(jax.nn.gelu(v_ref[rows, :])) * lnw_ref[...]).astype(BF16)
        for g in range(SM_GROUPS):
            cols = slice(g * LANES, (g + 1) * LANES)
            sg = jnp.dot(ws_ref[g], vn[:, cols], preferred_element_type=F32) + bs_ref[:, g:g + 1]
            o_ref[rows, cols] = (u[:, cols] * sg).astype(o_ref.dtype)


def _spatial_gating(z3, sm_ln_w, sm_w16, sm_bT, layer):
    B, L, _ = z3.shape
    ts = 2 * CHUNK
    return pl.pallas_call(
        functools.partial(_sg_kernel, n_sub=ts // CHUNK),
        out_shape=jax.ShapeDtypeStruct((B, L, D_SM), BF16),
        grid=(B, L // ts),
        in_specs=[
            pl.BlockSpec((None, ts, D_SM), lambda b, i: (b, i, COL_SU)),
            pl.BlockSpec((None, ts, D_SM), lambda b, i: (b, i, COL_SV)),
            pl.BlockSpec((None, 1, D_SM), lambda b, i: (layer, 0, 0)),
            pl.BlockSpec((None, SM_GROUPS, CHUNK, CHUNK), lambda b, i: (layer, 0, 0, 0)),
            pl.BlockSpec((None, CHUNK, SM_GROUPS), lambda b, i: (layer, 0, 0)),
        ],
        out_specs=pl.BlockSpec((None, ts, D_SM), lambda b, i: (b, i, 0)),
        compiler_params=_cparams(("arbitrary", "arbitrary"), 40),
        name="spatial_gating",
    )(z3, z3, sm_ln_w.reshape(DEPTH, 1, D_SM), sm_w16, sm_bT)


def _rms_rows(x, w):
    return x * lax.rsqrt(jnp.mean(x * x, axis=-1, keepdims=True) + RMS_EPS) * w


def _qk_kernel(*refs, rope):
    if rope:
        q_ref, k_ref, qw_ref, kw_ref, cos_ref, sin_ref, qo_ref, ko_ref = refs
        cos = cos_ref[...]
        sin = sin_ref[...]
        lane = lax.broadcasted_iota(jnp.int32, cos.shape, 1)
        first_half = (lane % (2 * ROPE_PAIRS)) < ROPE_PAIRS
    else:
        q_ref, k_ref, qw_ref, kw_ref, qo_ref, ko_ref, kf_ref = refs

    def rot(x):
        if not rope:
            return x
        partner = jnp.where(first_half, pltpu.roll(x, LANES - ROPE_PAIRS, axis=1), pltpu.roll(x, ROPE_PAIRS, axis=1))
        return x * cos + partner * sin

    for hd in range(H_ATT):
        cols = slice(hd * HEAD_DIM, (hd + 1) * HEAD_DIM)
        qo_ref[:, cols] = (rot(_rms_rows(q_ref[:, cols], qw_ref[...])) * Q_PRESCALE).astype(qo_ref.dtype)
    for hd in range(KV_HEADS):
        cols = slice(hd * HEAD_DIM, (hd + 1) * HEAD_DIM)
        kn = _rms_rows(k_ref[:, cols], kw_ref[...])
        if not rope:
            kf_ref[:, cols] = kn
        ko_ref[:, cols] = rot(kn).astype(ko_ref.dtype)


def _qk_prep(z3, q_norm_w, k_norm_w, layer, rope_tabs):
    B, L, _ = z3.shape
    tl = 256
    rope = rope_tabs is not None
    in_specs = [
        pl.BlockSpec((None, tl, D_ATT), lambda b, i: (b, i, COL_AQ)),
        pl.BlockSpec((None, tl, D_KV), lambda b, i: (b, i, COL_AK)),
        pl.BlockSpec((None, 1, HEAD_DIM), lambda b, i: (layer, 0, 0)),
        pl.BlockSpec((None, 1, HEAD_DIM), lambda b, i: (layer, 0, 0)),
    ]
    args = [z3, z3, q_norm_w.reshape(DEPTH, 1, HEAD_DIM), k_norm_w.reshape(DEPTH, 1, HEAD_DIM)]
    out_shape = [jax.ShapeDtypeStruct((B, L, D_ATT), BF16), jax.ShapeDtypeStruct((B, L, D_KV), BF16)]
    out_specs = [pl.BlockSpec((None, tl, D_ATT), lambda b, i: (b, i, 0)),
                 pl.BlockSpec((None, tl, D_KV), lambda b, i: (b, i, 0))]
    if rope:
        in_specs += [pl.BlockSpec((tl, HEAD_DIM), lambda b, i: (i, 0))] * 2
        args += list(rope_tabs)
    else:
        out_shape.append(jax.ShapeDtypeStruct((B, L, D_KV), F32))
        out_specs.append(pl.BlockSpec((None, tl, D_KV), lambda b, i: (b, i, 0)))
    return pl.pallas_call(
        functools.partial(_qk_kernel, rope=rope),
        out_shape=out_shape,
        grid=(B, L // tl),
        in_specs=in_specs,
        out_specs=out_specs,
        compiler_params=_cparams(("arbitrary", "arbitrary"), 40),
        name="qk_prep",
    )(*args)


def _rope_tables(n_tokens):
    rows = n_tokens // GRID_W
    row = jnp.broadcast_to(jnp.arange(rows, dtype=F32)[:, None], (rows, GRID_W)).reshape(-1)
    col = jnp.broadcast_to(jnp.arange(GRID_W, dtype=F32)[None, :], (rows, GRID_W)).reshape(-1)
    inv = ROPE_THETA ** (-2.0 * jnp.arange(ROPE_PAIRS, dtype=F32) / (2 * ROPE_PAIRS))
    ang_r = row[:, None] * inv
    ang_c = col[:, None] * inv
    cos = jnp.concatenate([jnp.cos(ang_r), jnp.cos(ang_r), jnp.cos(ang_c), jnp.cos(ang_c)], axis=1)
    sin = jnp.concatenate([-jnp.sin(ang_r), jnp.sin(ang_r), -jnp.sin(ang_c), jnp.sin(ang_c)], axis=1)
    return cos, sin


def _att_kernel(q_ref, k_ref, v1_ref, o_ref):
    k = k_ref[...]
    v1 = v1_ref[...]
    nt = (((1,), (1,)), ((), ()))
    for g in range(Q_PER_KV):
        cols = slice(g * HEAD_DIM, (g + 1) * HEAD_DIM)
        for r0 in range(0, q_ref.shape[0], ATT_UNIT_ROWS):
            rows = slice(r0, r0 + ATT_UNIT_ROWS)
            s = lax.dot_general(q_ref[rows, cols], k, nt, preferred_element_type=F32)
            p = jnp.exp2(s - jnp.max(s, axis=-1, keepdims=True)).astype(BF16)
            ol = jnp.dot(p, v1, preferred_element_type=F32)
            o_ref[rows, cols] = (ol[:, :HEAD_DIM] / ol[:, HEAD_DIM:]).astype(o_ref.dtype)


ATT_UNIT_ROWS = 256


def _attention(q, keys, vals):
    B, L, _ = q.shape
    Lk = keys.shape[1]
    tq = min(2 * ATT_UNIT_ROWS, L)
    blk = Q_PER_KV * HEAD_DIM
    v4 = vals.reshape(B, Lk, KV_HEADS, HEAD_DIM)
    vals1 = jnp.concatenate([v4, jnp.ones_like(v4)], axis=-1).reshape(B, Lk, 2 * D_KV)
    return pl.pallas_call(
        _att_kernel,
        out_shape=jax.ShapeDtypeStruct((B, L, D_ATT), BF16),
        grid=(B, KV_HEADS, L // tq),
        in_specs=[
            pl.BlockSpec((None, tq, blk), lambda b, h, i: (b, i, h)),
            pl.BlockSpec((None, Lk, HEAD_DIM), lambda b, h, i: (b, 0, h)),
            pl.BlockSpec((None, Lk, 2 * HEAD_DIM), lambda b, h, i: (b, 0, h)),
        ],
        out_specs=pl.BlockSpec((None, tq, blk), lambda b, h, i: (b, i, h)),
        compiler_params=_cparams(("arbitrary", "arbitrary", "arbitrary"), 48),
        name="attention",
    )(q, keys, vals1)


def _ln_kernel(*refs, n_y, gate_idx, mod_idx, with_router):
    x_ref = refs[0]
    y_refs = refs[1:1 + n_y]
    mg_ref, lnw_ref, lnb_ref = refs[1 + n_y:4 + n_y]
    rest = list(refs[4 + n_y:])
    y = y_refs[0][...]
    for r in y_refs[1:]:
        y = y + r[...]
    t = DEEPNORM_ALPHA * x_ref[...] + mg_ref[gate_idx:gate_idx + 1, :] * y
    xn = _normalize_rows(t) * lnw_ref[...] + lnb_ref[...]
    if mod_idx is None:
        (xo_ref,) = rest
        xo_ref[...] = xn
        return
    mh_ref = rest.pop(0)
    h = xn * (1.0 + mh_ref[mod_idx + 1:mod_idx + 2, :]) + mh_ref[mod_idx:mod_idx + 1, :]
    if with_router:
        wr_ref, br_ref, xo_ref, ho_ref, lg_ref = rest
        lg_ref[...] = jnp.dot(h, wr_ref[...], preferred_element_type=F32,
                              precision=lax.Precision.HIGHEST) + br_ref[...]
    else:
        xo_ref, ho_ref = rest
    xo_ref[...] = xn
    ho_ref[...] = h.astype(ho_ref.dtype)


def _ln_block(x, ys, mods_gate, gate_idx, ln_w, ln_b, layer, mods_h=None, mod_idx=None, router=None):
    B, L, D = x.shape
    tl = LN_ROWS
    row_spec = pl.BlockSpec((None, tl, D), lambda b, i: (b, i, 0))
    in_specs = [row_spec]
    args = [x]
    for arr, lead in ys:
        in_specs.append(pl.BlockSpec((None,) * len(lead) + (None, tl, D),
                                     functools.partial(lambda b, i, lead: lead + (b, i, 0), lead=lead)))
        args.append(arr)
    vec_spec = pl.BlockSpec((None, 1, D), lambda b, i: (layer, 0, 0))
    in_specs += [pl.BlockSpec((None, N_MOD, D), _mods_index(mods_gate.shape[0])), vec_spec, vec_spec]
    args += [mods_gate, ln_w.reshape(DEPTH, 1, D), ln_b.reshape(DEPTH, 1, D)]
    out_shape = [jax.ShapeDtypeStruct((B, L, D), F32)]
    out_specs = [row_spec]
    if mod_idx is not None:
        in_specs.append(pl.BlockSpec((None, N_MOD, D), _mods_index(mods_h.shape[0])))
        args.append(mods_h)
        if router is not None:
            w_r, b_r = router
            in_specs += [pl.BlockSpec((None, D, ROUTER_PAD), lambda b, i: (layer, 0, 0)),
                         pl.BlockSpec((None, 1, ROUTER_PAD), lambda b, i: (layer, 0, 0))]
            args += [w_r, b_r]
            out_shape += [jax.ShapeDtypeStruct((B, L, D), F32), jax.ShapeDtypeStruct((B, L, ROUTER_PAD), F32)]
            out_specs += [row_spec, pl.BlockSpec((None, tl, ROUTER_PAD), lambda b, i: (b, i, 0))]
        else:
            out_shape.append(jax.ShapeDtypeStruct((B, L, D), BF16))
            out_specs.append(row_spec)
    return pl.pallas_call(
        functools.partial(_ln_kernel, n_y=len(ys), gate_idx=gate_idx, mod_idx=mod_idx,
                          with_router=router is not None),
        out_shape=out_shape,
        grid=(B, L // tl),
        in_specs=in_specs,
        out_specs=out_specs,
        compiler_params=_cparams(("arbitrary", "arbitrary"), 48),
        name="layernorm",
    )(*args)


def _gather_kernel(idx_ref, src_ref, o_ref, buf_ref, sem_ref, *, rows):
    b = pl.program_id(0)
    nb = pl.num_programs(0)

    def issue(block, slot):
        base = block * rows

        def body(r, carry):
            t = idx_ref[base + r]
            pltpu.make_async_copy(src_ref.at[pl.ds(t, 1), :], buf_ref.at[slot, pl.ds(r, 1), :],
                                  sem_ref.at[slot]).start()
            return carry

        lax.fori_loop(0, rows, body, 0)

    slot = b % 2

    @pl.when(b == 0)
    def _():
        issue(0, 0)

    @pl.when(b + 1 < nb)
    def _():
        issue(b + 1, 1 - slot)

    pltpu.make_async_copy(src_ref.at[pl.ds(0, rows), :], buf_ref.at[slot], sem_ref.at[slot]).wait()
    o_ref[...] = buf_ref[slot].astype(o_ref.dtype)


def _gather_rows(src, idx, out_dtype):
    M = idx.shape[0]
    D = src.shape[1]
    rows = MOE_ROWS
    grid_spec = pltpu.PrefetchScalarGridSpec(
        num_scalar_prefetch=1,
        grid=(M // rows,),
        in_specs=[pl.BlockSpec(memory_space=pl.ANY)],
        out_specs=pl.BlockSpec((rows, D), lambda b, idx: (b, 0)),
        scratch_shapes=[pltpu.VMEM((2, rows, D), src.dtype), pltpu.SemaphoreType.DMA((2,))],
    )
    return pl.pallas_call(
        functools.partial(_gather_kernel, rows=rows),
        out_shape=jax.ShapeDtypeStruct((M, D), out_dtype),
        grid_spec=grid_spec,
        compiler_params=_cparams(("arbitrary",), 40),
        name="gather_rows",
    )(idx, src)


def _up_kernel(be_ref, nu_ref, x_ref, wg_ref, wu_ref, o_ref):
    b = pl.program_id(0)

    @pl.when(b < nu_ref[0])
    def _():
        x = x_ref[...]
        hg = jnp.dot(x, wg_ref[...], preferred_element_type=F32)
        hu = jnp.dot(x, wu_ref[...], preferred_element_type=F32)
        o_ref[...] = (_silu(hg) * hu).astype(o_ref.dtype)

    @pl.when(b >= nu_ref[0])
    def _():
        o_ref[...] = jnp.zeros_like(o_ref)


def _down_kernel(be_ref, nu_ref, a_ref, wd_ref, sw_ref, o_ref):
    b = pl.program_id(0)

    @pl.when(b < nu_ref[0])
    def _():
        o_ref[...] = jnp.dot(a_ref[...], wd_ref[...], preferred_element_type=F32) * sw_ref[...]

    @pl.when(b >= nu_ref[0])
    def _():
        o_ref[...] = jnp.zeros_like(o_ref)


def _experts(xb, block_e, n_used, slot_w, w_up16, w_down16, layer):
    n_slots = xb.shape[0]
    nb = n_slots // MOE_ROWS
    up_spec = pltpu.PrefetchScalarGridSpec(
        num_scalar_prefetch=2,
        grid=(nb,),
        in_specs=[
            pl.BlockSpec((MOE_ROWS, D_MODEL), lambda b, be, nu: (b, 0)),
            pl.BlockSpec((None, None, D_MODEL, D_EXPERT), lambda b, be, nu: (layer, be[b], 0, 0)),
            pl.BlockSpec((None, None, D_MODEL, D_EXPERT), lambda b, be, nu: (layer, be[b], 0, 1)),
        ],
        out_specs=pl.BlockSpec((MOE_ROWS, D_EXPERT), lambda b, be, nu: (b, 0)),
    )
    act = pl.pallas_call(
        _up_kernel,
        out_shape=jax.ShapeDtypeStruct((n_slots, D_EXPERT), BF16),
        grid_spec=up_spec,
        compiler_params=_cparams(("arbitrary",), 52),
        name="expert_up",
    )(block_e, n_used, xb, w_up16, w_up16)
    down_spec = pltpu.PrefetchScalarGridSpec(
        num_scalar_prefetch=2,
        grid=(nb,),
        in_specs=[
            pl.BlockSpec((MOE_ROWS, D_EXPERT), lambda b, be, nu: (b, 0)),
            pl.BlockSpec((None, None, D_EXPERT, D_MODEL), lambda b, be, nu: (layer, be[b], 0, 0)),
            pl.BlockSpec((MOE_ROWS, 1), lambda b, be, nu: (b, 0)),
        ],
        out_specs=pl.BlockSpec((MOE_ROWS, D_MODEL), lambda b, be, nu: (b, 0)),
    )
    return pl.pallas_call(
        _down_kernel,
        out_shape=jax.ShapeDtypeStruct((n_slots, D_MODEL), F32),
        grid_spec=down_spec,
        compiler_params=_cparams(("arbitrary",), 48),
        name="expert_down",
    )(block_e, n_used, act, w_down16, slot_w.reshape(n_slots, 1))


def _route(logits):
    T = logits.shape[0]
    lc = logits[:, :N_GROUPS]
    lf = logits[:, N_GROUPS:N_GROUPS + N_EXPERTS].reshape(T, N_GROUPS, EXPERTS_PER_GROUP)
    grp = jnp.argmax(lc, axis=-1).astype(jnp.int32)
    p_grp = jnp.take_along_axis(jax.nn.softmax(lc, axis=-1), grp[:, None], axis=-1)
    lfg = jnp.take_along_axis(lf, grp[:, None, None], axis=1)[:, 0]
    top_v, top_i = lax.top_k(lfg, 2)
    gate = jax.nn.softmax(top_v, axis=-1) * p_grp
    eid = (grp[:, None] * EXPERTS_PER_GROUP + top_i.astype(jnp.int32)).T.reshape(-1)
    wgt = gate.T.reshape(-1)
    n_assign = 2 * T
    experts = jnp.arange(N_EXPERTS, dtype=jnp.int32)
    counts = jnp.sum(eid[:, None] == experts[None, :], axis=0, dtype=jnp.int32)
    start = jnp.cumsum(counts) - counts
    pcounts = (counts + MOE_ROWS - 1) // MOE_ROWS * MOE_ROWS
    pend = jnp.cumsum(pcounts)
    pstart = pend - pcounts
    order = jnp.argsort(eid).astype(jnp.int32)
    rank = jnp.argsort(order).astype(jnp.int32)
    dest = rank + (pstart - start)[eid]
    n_blocks = n_assign // MOE_ROWS + N_EXPERTS
    block_first = jnp.arange(n_blocks, dtype=jnp.int32) * MOE_ROWS
    block_e = jnp.minimum(jnp.sum(pend[None, :] <= block_first[:, None], axis=1, dtype=jnp.int32), N_EXPERTS - 1)
    r = (block_first - pstart[block_e])[:, None] + jnp.arange(MOE_ROWS, dtype=jnp.int32)[None, :]
    valid = r < counts[block_e][:, None]
    src = order[jnp.clip(start[block_e][:, None] + r, 0, n_assign - 1)]
    slot = block_first[:, None] + jnp.arange(MOE_ROWS, dtype=jnp.int32)[None, :]
    slot_tok = jnp.where(valid, src % T, slot % T).reshape(-1)
    slot_w = jnp.where(valid, wgt[src], 0.0).reshape(-1)
    n_used = (pend[-1:] // MOE_ROWS).astype(jnp.int32)
    return slot_tok, slot_w, block_e, n_used, dest


def _layer(x, h, mods, mods_next, W, layer, ctx, rope_tabs):
    B, L, D = x.shape
    T = B * L
    z = _matmul([h.reshape(T, D)], W['w_in'], layer, (D,), F32)
    z3 = z.reshape(B, L, N_IN)

    ret, s_new = _retention(z3, W['log_decay'], ctx[0] if ctx else None, W['ret_gn_w'], layer, ctx is not None)
    sm = _spatial_gating(z3, W['sm_ln_w'], W['sm_w'], W['sm_bT'], layer)
    v_new = z3[:, :, OFF_AV:OFF_AV + D_KV]
    if ctx is None:
        q16, k16, k_new = _qk_prep(z3, W['q_norm_w'], W['k_norm_w'], layer, None)
        keys, vals = k16, v_new.astype(BF16)
        new_ctx = (s_new, k_new.reshape(B, L, KV_HEADS, HEAD_DIM), v_new.reshape(B, L, KV_HEADS, HEAD_DIM))
    else:
        q16, k16 = _qk_prep(z3, W['q_norm_w'], W['k_norm_w'], layer, rope_tabs)
        keys = jnp.concatenate([k16, ctx[1]], axis=1)
        vals = jnp.concatenate([v_new.astype(BF16), ctx[2]], axis=1)
        new_ctx = None
    att = _attention(q16, keys, vals)

    mix = _matmul([ret.reshape(T, D_RET), sm.reshape(T, D_SM), att.reshape(T, D_ATT)], W['w_out'], layer,
                  (D_RET, D_SM, D_ATT), F32)
    x1, h2, logits = _ln_block(x, [(mix.reshape(B, L, D), ())], mods, 2, W['ln1_w'], W['ln1_b'], layer,
                               mods_h=mods, mod_idx=3, router=(W['router_w'], W['router_b']))

    slot_tok, slot_w, block_e, n_used, dest = _route(logits.reshape(T, ROUTER_PAD))
    xb = _gather_rows(h2.reshape(T, D), slot_tok, BF16)
    yb = _experts(xb, block_e, n_used, slot_w, W['w_up'], W['w_down'], layer)
    ffn = _gather_rows(yb, dest, F32).reshape(2, B, L, D)
    ys = [(ffn, (0,)), (ffn, (1,))]
    if mods_next is None:
        (x2,) = _ln_block(x1, ys, mods, 5, W['ln2_w'], W['ln2_b'], layer)
        h_next = None
    else:
        x2, h_next = _ln_block(x1, ys, mods, 5, W['ln2_w'], W['ln2_b'], layer, mods_h=mods_next, mod_idx=0)
    return x2, h_next, new_ctx


def kernel(x_prompt, x_sample, state_ret, cache_k, cache_v, c, c_ctx, w_mod, b_mod, w_in, ret_log_decay, ret_gn_w, sm_ln_w, sm_w, sm_b, q_norm_w, k_norm_w, w_out, ln1_w, ln1_b, router_w_coarse, router_b_coarse, router_w_fine, router_b_fine, w_up, w_down, ln2_w, ln2_b):
    n_dec = c.shape[0]
    cond = jnp.zeros((16, D_MODEL), F32).at[:n_dec].set(c).at[n_dec].set(c_ctx)
    mods_all = _adaln_mods(cond, w_mod, b_mod)
    mods_lat = [mods_all[l, :n_dec].reshape(n_dec, N_MOD, D_MODEL) for l in range(DEPTH)]
    mods_ctx = [mods_all[l, n_dec:n_dec + 1].reshape(1, N_MOD, D_MODEL) for l in range(DEPTH)]

    n_r = N_GROUPS + N_EXPERTS
    router_w = jnp.zeros((DEPTH, D_MODEL, ROUTER_PAD), F32).at[:, :, :n_r].set(
        jnp.concatenate([router_w_coarse, router_w_fine], axis=-1))
    router_b = jnp.zeros((DEPTH, 1, ROUTER_PAD), F32).at[:, 0, :n_r].set(
        jnp.concatenate([router_b_coarse, router_b_fine], axis=-1))
    W = {
        'w_in': w_in.astype(BF16), 'w_out': w_out.astype(BF16),
        'w_up': w_up.astype(BF16), 'w_down': w_down.astype(BF16),
        'log_decay': ret_log_decay.reshape(-1), 'ret_gn_w': ret_gn_w,
        'sm_ln_w': sm_ln_w, 'sm_w': sm_w.astype(BF16), 'sm_bT': jnp.swapaxes(sm_b, 1, 2),
        'q_norm_w': q_norm_w, 'k_norm_w': k_norm_w,
        'ln1_w': ln1_w, 'ln1_b': ln1_b, 'ln2_w': ln2_w, 'ln2_b': ln2_b,
        'router_w': router_w, 'router_b': router_b,
    }
    rope_tabs = _rope_tables(x_sample.shape[1])
    past = cache_k.shape[2]
    ck16 = cache_k.astype(BF16).reshape(n_dec, DEPTH, past, D_KV)
    cv16 = cache_v.astype(BF16).reshape(n_dec, DEPTH, past, D_KV)

    y_p, y_s = x_prompt, x_sample
    h_p = _modulate(x_prompt, mods_ctx[0])
    h_s = _modulate(x_sample, mods_lat[0])
    states, ctx_k, ctx_v = [], [], []
    for l in range(DEPTH):
        nxt = l + 1 < DEPTH
        y_p, h_p, (s_l, k_l, v_l) = _layer(y_p, h_p, mods_ctx[l], mods_ctx[l + 1] if nxt else None, W, l, None, None)
        states.append(s_l)
        ctx_k.append(k_l)
        ctx_v.append(v_l)
        y_s, h_s, _ = _layer(y_s, h_s, mods_lat[l], mods_lat[l + 1] if nxt else None, W, l,
                             (state_ret, ck16[:, l], cv16[:, l]), rope_tabs)
    return (y_p, y_s, jnp.stack(states, axis=1), jnp.stack(ctx_k, axis=1), jnp.stack(ctx_v, axis=1))
```

```python
import functools

import numpy as np
import jax
import jax.numpy as jnp
from jax import lax
from jax.experimental import pallas as pl
from jax.experimental.pallas import tpu as pltpu

F32 = jnp.float32
BF16 = jnp.bfloat16

D_MODEL = 4096
DEPTH = 2
GRID_W = 64
HEAD_DIM = 128
H_RET = 8
D_RET = H_RET * HEAD_DIM
SM_GROUPS = 8
D_SM = 1024
H_ATT = 16
KV_HEADS = 4
Q_PER_KV = H_ATT // KV_HEADS
D_ATT = H_ATT * HEAD_DIM
D_KV = KV_HEADS * HEAD_DIM
CHUNK = 128
ROPE_PAIRS = 32
ROPE_THETA = 10000.0
N_GROUPS = 4
EXPERTS_PER_GROUP = 8
N_EXPERTS = N_GROUPS * EXPERTS_PER_GROUP
D_EXPERT = D_MODEL // 4
N_MOD = 6
N_IN = D_RET * 4 + D_SM * 2 + D_ATT + D_KV * 2
DEEPNORM_ALPHA = (2 * DEPTH) ** 0.25
LN_EPS = 1e-5
RMS_EPS = 1e-6
Q_PRESCALE = float(HEAD_DIM ** -0.5 * np.log2(np.e))

COL_RQ, COL_RK, COL_RV, COL_RG = 0, H_RET, 2 * H_RET, 3 * H_RET
COL_SU, COL_SV = 4, 5
COL_AQ = 3
COL_AK, COL_AV = 16, 17
OFF_AV = COL_AV * D_KV

LANES = 128
ROUTER_PAD = LANES
MIB = 1024 * 1024
MOE_ROWS = 256
TM = 1024
TN = 1024
LN_ROWS = 128
MOD_ROWS = 512


def _cparams(sem, vmem_mib):
    return pltpu.CompilerParams(dimension_semantics=sem, vmem_limit_bytes=vmem_mib * MIB)


def _sigmoid(x):
    return 1.0 / (1.0 + jnp.exp(-x))


def _silu(x):
    return x * _sigmoid(x)


def _mods_kernel(c_ref, w_ref, b_ref, o_ref):
    a = _silu(c_ref[...]).astype(BF16)
    o_ref[...] = jnp.dot(a, w_ref[...].astype(BF16), preferred_element_type=F32) + b_ref[...]


def _adaln_mods(cond, w_mod, b_mod):
    n = cond.shape[0]
    tn = 512
    n_out = N_MOD * D_MODEL
    return pl.pallas_call(
        _mods_kernel,
        out_shape=jax.ShapeDtypeStruct((DEPTH, n, n_out), F32),
        grid=(DEPTH, n_out // tn),
        in_specs=[
            pl.BlockSpec((n, D_MODEL), lambda l, j: (0, 0)),
            pl.BlockSpec((None, D_MODEL, tn), lambda l, j: (l, 0, j)),
            pl.BlockSpec((None, 1, tn), lambda l, j: (l, 0, j)),
        ],
        out_specs=pl.BlockSpec((None, n, tn), lambda l, j: (l, 0, j)),
        compiler_params=_cparams(("arbitrary", "arbitrary"), 40),
        name="adaln_mods",
    )(cond, w_mod, b_mod.reshape(DEPTH, 1, n_out))


def _modulate_kernel(x_ref, m_ref, o_ref):
    o_ref[...] = (x_ref[...] * (1.0 + m_ref[1:2, :]) + m_ref[0:1, :]).astype(BF16)


def _mods_index(bm):
    if bm == 1:
        return lambda b, i: (0, 0, 0)
    return lambda b, i: (b, 0, 0)


def _modulate(x, mods):
    B, L, D = x.shape
    tl = min(MOD_ROWS, L)
    return pl.pallas_call(
        _modulate_kernel,
        out_shape=jax.ShapeDtypeStruct((B, L, D), BF16),
        grid=(B, L // tl),
        in_specs=[
            pl.BlockSpec((None, tl, D), lambda b, i: (b, i, 0)),
            pl.BlockSpec((None, N_MOD, D), _mods_index(mods.shape[0])),
        ],
        out_specs=pl.BlockSpec((None, tl, D), lambda b, i: (b, i, 0)),
        compiler_params=_cparams(("arbitrary", "arbitrary"), 40),
        name="modulate",
    )(x, mods)


def _mm_kernel(*refs, n_a):
    o_ref = refs[2 * n_a]
    acc = None
    for a_ref, w_ref in zip(refs[:n_a], refs[n_a:2 * n_a]):
        d = jnp.dot(a_ref[...], w_ref[...], preferred_element_type=F32)
        acc = d if acc is None else acc + d
    o_ref[...] = acc.astype(o_ref.dtype)


def _matmul(a_list, w, layer, k_sizes, out_dtype):
    T = a_list[0].shape[0]
    N = w.shape[2]
    tm = min(TM, T)
    in_specs = [pl.BlockSpec((tm, k), lambda i, j: (i, 0)) for k in k_sizes]
    row = 0
    for k in k_sizes:
        assert row % k == 0
        in_specs.append(pl.BlockSpec((None, k, TN), functools.partial(lambda i, j, rb: (layer, rb, j), rb=row // k)))
        row += k
    return pl.pallas_call(
        functools.partial(_mm_kernel, n_a=len(a_list)),
        out_shape=jax.ShapeDtypeStruct((T, N), out_dtype),
        grid=(T // tm, N // TN),
        in_specs=in_specs,
        out_specs=pl.BlockSpec((tm, TN), lambda i, j: (i, j)),
        compiler_params=_cparams(("arbitrary", "arbitrary"), 56),
        name="matmul",
    )(*a_list, *([w] * len(a_list)))


def _normalize_rows(x):
    mu = jnp.mean(x, axis=-1, keepdims=True)
    xc = x - mu
    var = jnp.mean(xc * xc, axis=-1, keepdims=True)
    return xc * lax.rsqrt(var + LN_EPS)


RET_HEADS = 2
RET_UNROLL = 4


def _ret_consts(lg_f, lg_b):
    C = CHUNK
    ii = lax.broadcasted_iota(jnp.int32, (C, C), 0).astype(F32)
    jj = lax.broadcasted_iota(jnp.int32, (C, C), 1).astype(F32)
    diff = ii - jj
    mask = (jnp.where(diff >= 0, jnp.exp(lg_f * jnp.maximum(diff, 0.0)), 0.0)
            + jnp.where(diff <= 0, jnp.exp(lg_b * jnp.maximum(-diff, 0.0)), 0.0))
    pos = lax.broadcasted_iota(jnp.int32, (C, 1), 0).astype(F32)
    return dict(
        mask=mask,
        qdec_f=jnp.exp(lg_f * (pos + 1.0)), kdec_f=jnp.exp(lg_f * (C - 1.0 - pos)),
        qdec_b=jnp.exp(lg_b * (C - pos)), kdec_b=jnp.exp(lg_b * pos),
        cdec_f=jnp.exp(jnp.full((1, HEAD_DIM), lg_f * C, F32)),
        cdec_b=jnp.exp(jnp.full((1, HEAD_DIM), lg_b * C, F32)))


def _ret_kernel(ld_ref, q_ref, k_ref, v_ref, g_ref, s0_ref, gnw_ref, o_ref, sout_ref, oacc_ref, *,
                layer, n_chunks, has_ctx):
    hb = pl.program_id(1)
    C = CHUNK
    consts = []
    for j in range(RET_HEADS):
        h = hb * RET_HEADS + j
        consts.append(_ret_consts(ld_ref[layer * 2 * H_RET + h], ld_ref[layer * 2 * H_RET + H_RET + h]))
    k_scale = HEAD_DIM ** -0.5
    nt = (((1,), (1,)), ((), ()))

    def load_chunk(c, j):
        rows = pl.ds(pl.multiple_of(c * C, C), C)
        cols = slice(j * HEAD_DIM, (j + 1) * HEAD_DIM)
        q = q_ref[rows, cols].astype(BF16)
        kf = k_ref[rows, cols] * k_scale
        v = v_ref[rows, cols].astype(BF16)
        return rows, cols, q, kf, v

    def state_update(s, kf, kdec, cdec, v):
        kd = (kf * kdec).astype(BF16)
        return s * cdec + jnp.dot(kd.T, v, preferred_element_type=F32)

    def fwd(c, states):
        out = []
        for j, s in enumerate(states):
            cj = consts[j]
            rows, cols, q, kf, v = load_chunk(c, j)
            qk = lax.dot_general(q, kf.astype(BF16), nt, preferred_element_type=F32)
            a = (qk * cj['mask']).astype(BF16)
            o = (jnp.dot(a, v, preferred_element_type=F32)
                 + jnp.dot(q, s.astype(BF16), preferred_element_type=F32) * cj['qdec_f'])
            oacc_ref[rows, cols] = o
            out.append(state_update(s, kf, cj['kdec_f'], cj['cdec_f'], v))
        return tuple(out)

    def bwd(t, states):
        c = n_chunks - 1 - t
        out = []
        for j, s in enumerate(states):
            cj = consts[j]
            rows, cols, q, kf, v = load_chunk(c, j)
            o = oacc_ref[rows, cols] + jnp.dot(q, s.astype(BF16), preferred_element_type=F32) * cj['qdec_b']
            y = _normalize_rows(o) * gnw_ref[:, cols] * _silu(g_ref[rows, cols])
            o_ref[rows, cols] = y.astype(o_ref.dtype)
            out.append(state_update(s, kf, cj['kdec_b'], cj['cdec_b'], v))
        return tuple(out)

    if has_ctx:
        s0_f = tuple(s0_ref[0, j] for j in range(RET_HEADS))
        s0_b = tuple(s0_ref[1, j] for j in range(RET_HEADS))
    else:
        s0_f = tuple(jnp.zeros((HEAD_DIM, HEAD_DIM), F32) for _ in range(RET_HEADS))
        s0_b = s0_f
    unroll = min(RET_UNROLL, n_chunks)
    s_f = lax.fori_loop(0, n_chunks, fwd, s0_f, unroll=unroll)
    s_b = lax.fori_loop(0, n_chunks, bwd, s0_b, unroll=unroll)
    for j in range(RET_HEADS):
        sout_ref[0, j] = s_f[j]
        sout_ref[1, j] = s_b[j]


def _retention(z3, log_decay_flat, state_ret, gn_w, layer, has_ctx):
    B, L, _ = z3.shape
    hp = RET_HEADS
    width = hp * HEAD_DIM
    col = lambda off: pl.BlockSpec((None, L, width), lambda b, h, ld: (b, 0, off // hp + h))
    state_block = (None, None, 2, hp, HEAD_DIM, HEAD_DIM)
    if has_ctx:
        s0 = state_ret
        s0_spec = pl.BlockSpec(state_block, lambda b, h, ld: (b, layer, 0, h, 0, 0))
    else:
        s0 = jnp.zeros((1, 1, 2, hp, HEAD_DIM, HEAD_DIM), F32)
        s0_spec = pl.BlockSpec(state_block, lambda b, h, ld: (0, 0, 0, 0, 0, 0))
    grid_spec = pltpu.PrefetchScalarGridSpec(
        num_scalar_prefetch=1,
        grid=(B, H_RET // hp),
        in_specs=[col(COL_RQ), col(COL_RK), col(COL_RV), col(COL_RG), s0_spec,
                  pl.BlockSpec((None, 1, width), lambda b, h, ld: (layer, 0, h))],
        out_specs=[pl.BlockSpec((None, L, width), lambda b, h, ld: (b, 0, h)),
                   pl.BlockSpec((None, 2, hp, HEAD_DIM, HEAD_DIM), lambda b, h, ld: (b, 0, h, 0, 0))],
        scratch_shapes=[pltpu.VMEM((L, width), F32)],
    )
    return pl.pallas_call(
        functools.partial(_ret_kernel, layer=layer, n_chunks=L // CHUNK, has_ctx=has_ctx),
        out_shape=[jax.ShapeDtypeStruct((B, L, D_RET), BF16),
                   jax.ShapeDtypeStruct((B, 2, H_RET, HEAD_DIM, HEAD_DIM), F32)],
        grid_spec=grid_spec,
        compiler_params=_cparams(("arbitrary", "arbitrary"), 52),
        name="retention",
    )(log_decay_flat, z3, z3, z3, z3, s0, gn_w.reshape(DEPTH, 1, D_RET))


def _sg_kernel(u_ref, v_ref, lnw_ref, ws_ref, bs_ref, o_ref, *, n_sub):
    for s in range(n_sub):
        rows = slice(s * CHUNK, (s + 1) * CHUNK)
        u = jax.nn.gelu(u_ref[rows, :])
        vn = (_normalize_rows(jax.nn.gelu(v_ref[rows, :])) * lnw_ref[...]).astype(BF16)
        for g in range(SM_GROUPS):
            cols = slice(g * LANES, (g + 1) * LANES)
            sg = jnp.dot(ws_ref[g], vn[:, cols], preferred_element_type=F32) + bs_ref[:, g:g + 1]
            o_ref[rows, cols] = (u[:, cols] * sg).astype(o_ref.dtype)


def _spatial_gating(z3, sm_ln_w, sm_w16, sm_bT, layer):
    B, L, _ = z3.shape
    ts = 2 * CHUNK
    return pl.pallas_call(
        functools.partial(_sg_kernel, n_sub=ts // CHUNK),
        out_shape=jax.ShapeDtypeStruct((B, L, D_SM), BF16),
        grid=(B, L // ts),
        in_specs=[
            pl.BlockSpec((None, ts, D_SM), lambda b, i: (b, i, COL_SU)),
            pl.BlockSpec((None, ts, D_SM), lambda b, i: (b, i, COL_SV)),
            pl.BlockSpec((None, 1, D_SM), lambda b, i: (layer, 0, 0)),
            pl.BlockSpec((None, SM_GROUPS, CHUNK, CHUNK), lambda b, i: (layer, 0, 0, 0)),
            pl.BlockSpec((None, CHUNK, SM_GROUPS), lambda b, i: (layer, 0, 0)),
        ],
        out_specs=pl.BlockSpec((None, ts, D_SM), lambda b, i: (b, i, 0)),
        compiler_params=_cparams(("arbitrary", "arbitrary"), 40),
        name="spatial_gating",
    )(z3, z3, sm_ln_w.reshape(DEPTH, 1, D_SM), sm_w16, sm_bT)


def _rms_rows(x, w):
    return x * lax.rsqrt(jnp.mean(x * x, axis=-1, keepdims=True) + RMS_EPS) * w


def _qk_kernel(*refs, rope):
    if rope:
        q_ref, k_ref, qw_ref, kw_ref, cos_ref, sin_ref, qo_ref, ko_ref = refs
        cos = cos_ref[...]
        sin = sin_ref[...]
        lane = lax.broadcasted_iota(jnp.int32, cos.shape, 1)
        first_half = (lane % (2 * ROPE_PAIRS)) < ROPE_PAIRS
    else:
        q_ref, k_ref, qw_ref, kw_ref, qo_ref, ko_ref, kf_ref = refs

    def rot(x):
        if not rope:
            return x
        partner = jnp.where(first_half, pltpu.roll(x, LANES - ROPE_PAIRS, axis=1), pltpu.roll(x, ROPE_PAIRS, axis=1))
        return x * cos + partner * sin

    for hd in range(H_ATT):
        cols = slice(hd * HEAD_DIM, (hd + 1) * HEAD_DIM)
        qo_ref[:, cols] = (rot(_rms_rows(q_ref[:, cols], qw_ref[...])) * Q_PRESCALE).astype(qo_ref.dtype)
    for hd in range(KV_HEADS):
        cols = slice(hd * HEAD_DIM, (hd + 1) * HEAD_DIM)
        kn = _rms_rows(k_ref[:, cols], kw_ref[...])
        if not rope:
            kf_ref[:, cols] = kn
        ko_ref[:, cols] = rot(kn).astype(ko_ref.dtype)


def _qk_prep(z3, q_norm_w, k_norm_w, layer, rope_tabs):
    B, L, _ = z3.shape
    tl = 256
    rope = rope_tabs is not None
    in_specs = [
        pl.BlockSpec((None, tl, D_ATT), lambda b, i: (b, i, COL_AQ)),
        pl.BlockSpec((None, tl, D_KV), lambda b, i: (b, i, COL_AK)),
        pl.BlockSpec((None, 1, HEAD_DIM), lambda b, i: (layer, 0, 0)),
        pl.BlockSpec((None, 1, HEAD_DIM), lambda b, i: (layer, 0, 0)),
    ]
    args = [z3, z3, q_norm_w.reshape(DEPTH, 1, HEAD_DIM), k_norm_w.reshape(DEPTH, 1, HEAD_DIM)]
    out_shape = [jax.ShapeDtypeStruct((B, L, D_ATT), BF16), jax.ShapeDtypeStruct((B, L, D_KV), BF16)]
    out_specs = [pl.BlockSpec((None, tl, D_ATT), lambda b, i: (b, i, 0)),
                 pl.BlockSpec((None, tl, D_KV), lambda b, i: (b, i, 0))]
    if rope:
        in_specs += [pl.BlockSpec((tl, HEAD_DIM), lambda b, i: (i, 0))] * 2
        args += list(rope_tabs)
    else:
        out_shape.append(jax.ShapeDtypeStruct((B, L, D_KV), F32))
        out_specs.append(pl.BlockSpec((None, tl, D_KV), lambda b, i: (b, i, 0)))
    return pl.pallas_call(
        functools.partial(_qk_kernel, rope=rope),
        out_shape=out_shape,
        grid=(B, L // tl),
        in_specs=in_specs,
        out_specs=out_specs,
        compiler_params=_cparams(("arbitrary", "arbitrary"), 40),
        name="qk_prep",
    )(*args)


def _rope_tables(n_tokens):
    rows = n_tokens // GRID_W
    row = jnp.broadcast_to(jnp.arange(rows, dtype=F32)[:, None], (rows, GRID_W)).reshape(-1)
    col = jnp.broadcast_to(jnp.arange(GRID_W, dtype=F32)[None, :], (rows, GRID_W)).reshape(-1)
    inv = ROPE_THETA ** (-2.0 * jnp.arange(ROPE_PAIRS, dtype=F32) / (2 * ROPE_PAIRS))
    ang_r = row[:, None] * inv
    ang_c = col[:, None] * inv
    cos = jnp.concatenate([jnp.cos(ang_r), jnp.cos(ang_r), jnp.cos(ang_c), jnp.cos(ang_c)], axis=1)
    sin = jnp.concatenate([-jnp.sin(ang_r), jnp.sin(ang_r), -jnp.sin(ang_c), jnp.sin(ang_c)], axis=1)
    return cos, sin


def _att_kernel(q_ref, k_ref, v1_ref, o_ref):
    k = k_ref[...]
    v1 = v1_ref[...]
    nt = (((1,), (1,)), ((), ()))
    for g in range(Q_PER_KV):
        cols = slice(g * HEAD_DIM, (g + 1) * HEAD_DIM)
        for r0 in range(0, q_ref.shape[0], ATT_UNIT_ROWS):
            rows = slice(r0, r0 + ATT_UNIT_ROWS)
            s = lax.dot_general(q_ref[rows, cols], k, nt, preferred_element_type=F32)
            p = jnp.exp2(s - jnp.max(s, axis=-1, keepdims=True)).astype(BF16)
            ol = jnp.dot(p, v1, preferred_element_type=F32)
            o_ref[rows, cols] = (ol[:, :HEAD_DIM] / ol[:, HEAD_DIM:]).astype(o_ref.dtype)


ATT_UNIT_ROWS = 256
ATT_UNITS_PER_HEAD = 4


def _attention(q, keys, vals):
    B, L, _ = q.shape
    Lk = keys.shape[1]
    tq = min(ATT_UNITS_PER_HEAD * ATT_UNIT_ROWS, L)
    blk = Q_PER_KV * HEAD_DIM
    v4 = vals.reshape(B, Lk, KV_HEADS, HEAD_DIM)
    vals1 = jnp.concatenate([v4, jnp.ones_like(v4)], axis=-1).reshape(B, Lk, 2 * D_KV)
    return pl.pallas_call(
        _att_kernel,
        out_shape=jax.ShapeDtypeStruct((B, L, D_ATT), BF16),
        grid=(B, KV_HEADS, L // tq),
        in_specs=[
            pl.BlockSpec((None, tq, blk), lambda b, h, i: (b, i, h)),
            pl.BlockSpec((None, Lk, HEAD_DIM), lambda b, h, i: (b, 0, h)),
            pl.BlockSpec((None, Lk, 2 * HEAD_DIM), lambda b, h, i: (b, 0, h)),
        ],
        out_specs=pl.BlockSpec((None, tq, blk), lambda b, h, i: (b, i, h)),
        compiler_params=_cparams(("arbitrary", "arbitrary", "arbitrary"), 48),
        name="attention",
    )(q, keys, vals1)


def _ln_kernel(*refs, n_y, gate_idx, mod_idx, with_router):
    x_ref = refs[0]
    y_refs = refs[1:1 + n_y]
    mg_ref, lnw_ref, lnb_ref = refs[1 + n_y:4 + n_y]
    rest = list(refs[4 + n_y:])
    y = y_refs[0][...]
    for r in y_refs[1:]:
        y = y + r[...]
    t = DEEPNORM_ALPHA * x_ref[...] + mg_ref[gate_idx:gate_idx + 1, :] * y
    xn = _normalize_rows(t) * lnw_ref[...] + lnb_ref[...]
    if mod_idx is None:
        (xo_ref,) = rest
        xo_ref[...] = xn
        return
    mh_ref = rest.pop(0)
    h = xn * (1.0 + mh_ref[mod_idx + 1:mod_idx + 2, :]) + mh_ref[mod_idx:mod_idx + 1, :]
    if with_router:
        wr_ref, br_ref, xo_ref, ho_ref, lg_ref = rest
        lg_ref[...] = jnp.dot(h, wr_ref[...], preferred_element_type=F32,
                              precision=lax.Precision.HIGHEST) + br_ref[...]
    else:
        xo_ref, ho_ref = rest
    xo_ref[...] = xn
    ho_ref[...] = h.astype(ho_ref.dtype)


def _ln_block(x, ys, mods_gate, gate_idx, ln_w, ln_b, layer, mods_h=None, mod_idx=None, router=None):
    B, L, D = x.shape
    tl = LN_ROWS
    row_spec = pl.BlockSpec((None, tl, D), lambda b, i: (b, i, 0))
    in_specs = [row_spec]
    args = [x]
    for arr, first in ys:
        in_specs.append(pl.BlockSpec((tl, D), functools.partial(lambda b, i, first: (first + b * (L // tl) + i, 0),
                                                                first=first)))
        args.append(arr)
    vec_spec = pl.BlockSpec((None, 1, D), lambda b, i: (layer, 0, 0))
    in_specs += [pl.BlockSpec((None, N_MOD, D), _mods_index(mods_gate.shape[0])), vec_spec, vec_spec]
    args += [mods_gate, ln_w.reshape(DEPTH, 1, D), ln_b.reshape(DEPTH, 1, D)]
    out_shape = [jax.ShapeDtypeStruct((B, L, D), F32)]
    out_specs = [row_spec]
    if mod_idx is not None:
        in_specs.append(pl.BlockSpec((None, N_MOD, D), _mods_index(mods_h.shape[0])))
        args.append(mods_h)
        if router is not None:
            w_r, b_r = router
            in_specs += [pl.BlockSpec((None, D, ROUTER_PAD), lambda b, i: (layer, 0, 0)),
                         pl.BlockSpec((None, 1, ROUTER_PAD), lambda b, i: (layer, 0, 0))]
            args += [w_r, b_r]
            out_shape += [jax.ShapeDtypeStruct((B, L, D), F32), jax.ShapeDtypeStruct((B, L, ROUTER_PAD), F32)]
            out_specs += [row_spec, pl.BlockSpec((None, tl, ROUTER_PAD), lambda b, i: (b, i, 0))]
        else:
            out_shape.append(jax.ShapeDtypeStruct((B, L, D), BF16))
            out_specs.append(row_spec)
    return pl.pallas_call(
        functools.partial(_ln_kernel, n_y=len(ys), gate_idx=gate_idx, mod_idx=mod_idx,
                          with_router=router is not None),
        out_shape=out_shape,
        grid=(B, L // tl),
        in_specs=in_specs,
        out_specs=out_specs,
        compiler_params=_cparams(("arbitrary", "arbitrary"), 48),
        name="layernorm",
    )(*args)


def _up_kernel(be_ref, nu_ref, tok_ref, x_hbm, wg_ref, wu_ref, o_ref, xbuf, sem):
    b = pl.program_id(0)
    n_used = nu_ref[0]
    slot = b % 2

    def issue(block, to_slot):
        base = block * MOE_ROWS

        def body(r, carry):
            t = tok_ref[base + r]
            pltpu.make_async_copy(x_hbm.at[pl.ds(t, 1), :], xbuf.at[to_slot, pl.ds(r, 1), :],
                                  sem.at[to_slot]).start()
            return carry

        lax.fori_loop(0, MOE_ROWS, body, 0)

    @pl.when(b == 0)
    def _():
        issue(0, 0)

    @pl.when(b + 1 < n_used)
    def _():
        issue(b + 1, 1 - slot)

    @pl.when(b < n_used)
    def _():
        pltpu.make_async_copy(x_hbm.at[pl.ds(0, MOE_ROWS), :], xbuf.at[slot], sem.at[slot]).wait()
        x = xbuf[slot].astype(BF16)
        hg = jnp.dot(x, wg_ref[...], preferred_element_type=F32)
        hu = jnp.dot(x, wu_ref[...], preferred_element_type=F32)
        o_ref[...] = (_silu(hg) * hu).astype(o_ref.dtype)

    @pl.when(b >= n_used)
    def _():
        o_ref[...] = jnp.zeros_like(o_ref)


def _down_kernel(be_ref, nu_ref, dst_ref, a_ref, wd_ref, sw_ref, out_hbm, ybuf, sem, *, n_rows):
    b = pl.program_id(0)
    n_used = nu_ref[0]
    slot = b % 2

    def wait_block(of_slot):
        pltpu.make_async_copy(ybuf.at[of_slot], out_hbm.at[pl.ds(0, MOE_ROWS), :], sem.at[of_slot]).wait()

    @pl.when(b == 0)
    def _():
        ybuf[...] = jnp.zeros_like(ybuf)
        for s in range(2):
            fill = pltpu.make_async_copy(ybuf.at[s], out_hbm.at[pl.ds(n_rows + s * MOE_ROWS, MOE_ROWS), :], sem.at[s])
            fill.start()
            fill.wait()

    @pl.when(b < n_used)
    def _():
        ybuf[slot] = jnp.dot(a_ref[...], wd_ref[...], preferred_element_type=F32) * sw_ref[...]
        base = b * MOE_ROWS

        def body(r, carry):
            d = dst_ref[base + r]
            pltpu.make_async_copy(ybuf.at[slot, pl.ds(r, 1), :], out_hbm.at[pl.ds(d, 1), :], sem.at[slot]).start()
            return carry

        lax.fori_loop(0, MOE_ROWS, body, 0)

    @pl.when((b >= 1) & (b - 1 < n_used))
    def _():
        wait_block(1 - slot)

    @pl.when((b == pl.num_programs(0) - 1) & (b < n_used))
    def _():
        wait_block(slot)


def _experts(h2, slot_tok, slot_dst, block_e, n_used, slot_w, w_up16, w_down16, layer):
    T = h2.shape[0]
    n_slots = slot_tok.shape[0]
    nb = n_slots // MOE_ROWS
    up_spec = pltpu.PrefetchScalarGridSpec(
        num_scalar_prefetch=3,
        grid=(nb,),
        in_specs=[
            pl.BlockSpec(memory_space=pl.ANY),
            pl.BlockSpec((None, None, D_MODEL, D_EXPERT), lambda b, be, nu, tok: (layer, be[b], 0, 0)),
            pl.BlockSpec((None, None, D_MODEL, D_EXPERT), lambda b, be, nu, tok: (layer, be[b], 0, 1)),
        ],
        out_specs=pl.BlockSpec((MOE_ROWS, D_EXPERT), lambda b, be, nu, tok: (b, 0)),
        scratch_shapes=[pltpu.VMEM((2, MOE_ROWS, D_MODEL), F32), pltpu.SemaphoreType.DMA((2,))],
    )
    act = pl.pallas_call(
        _up_kernel,
        out_shape=jax.ShapeDtypeStruct((n_slots, D_EXPERT), BF16),
        grid_spec=up_spec,
        compiler_params=_cparams(("arbitrary",), 56),
        name="expert_up",
    )(block_e, n_used, slot_tok, h2, w_up16, w_up16)
    down_spec = pltpu.PrefetchScalarGridSpec(
        num_scalar_prefetch=3,
        grid=(nb,),
        in_specs=[
            pl.BlockSpec((MOE_ROWS, D_EXPERT), lambda b, be, nu, dst: (b, 0)),
            pl.BlockSpec((None, None, D_EXPERT, D_MODEL), lambda b, be, nu, dst: (layer, be[b], 0, 0)),
            pl.BlockSpec((MOE_ROWS, 1), lambda b, be, nu, dst: (b, 0)),
        ],
        out_specs=pl.BlockSpec(memory_space=pl.ANY),
        scratch_shapes=[pltpu.VMEM((2, MOE_ROWS, D_MODEL), F32), pltpu.SemaphoreType.DMA((2,))],
    )
    return pl.pallas_call(
        functools.partial(_down_kernel, n_rows=2 * T),
        out_shape=jax.ShapeDtypeStruct((2 * T + 2 * MOE_ROWS, D_MODEL), F32),
        grid_spec=down_spec,
        compiler_params=_cparams(("arbitrary",), 48),
        name="expert_down",
    )(block_e, n_used, slot_dst, act, w_down16, slot_w.reshape(n_slots, 1))


def _route(logits):
    T = logits.shape[0]
    lc = logits[:, :N_GROUPS]
    lf = logits[:, N_GROUPS:N_GROUPS + N_EXPERTS].reshape(T, N_GROUPS, EXPERTS_PER_GROUP)
    grp = jnp.argmax(lc, axis=-1).astype(jnp.int32)
    p_grp = jnp.take_along_axis(jax.nn.softmax(lc, axis=-1), grp[:, None], axis=-1)
    lfg = jnp.take_along_axis(lf, grp[:, None, None], axis=1)[:, 0]
    top_v, top_i = lax.top_k(lfg, 2)
    gate = jax.nn.softmax(top_v, axis=-1) * p_grp
    eid = (grp[:, None] * EXPERTS_PER_GROUP + top_i.astype(jnp.int32)).T.reshape(-1)
    wgt = gate.T.reshape(-1)
    n_assign = 2 * T
    experts = jnp.arange(N_EXPERTS, dtype=jnp.int32)
    counts = jnp.sum(eid[:, None] == experts[None, :], axis=0, dtype=jnp.int32)
    start = jnp.cumsum(counts) - counts
    pcounts = (counts + MOE_ROWS - 1) // MOE_ROWS * MOE_ROWS
    pend = jnp.cumsum(pcounts)
    pstart = pend - pcounts
    order = jnp.argsort(eid).astype(jnp.int32)
    n_blocks = n_assign // MOE_ROWS + N_EXPERTS
    block_first = jnp.arange(n_blocks, dtype=jnp.int32) * MOE_ROWS
    block_e = jnp.minimum(jnp.sum(pend[None, :] <= block_first[:, None], axis=1, dtype=jnp.int32), N_EXPERTS - 1)
    r = (block_first - pstart[block_e])[:, None] + jnp.arange(MOE_ROWS, dtype=jnp.int32)[None, :]
    valid = r < counts[block_e][:, None]
    src = order[jnp.clip(start[block_e][:, None] + r, 0, n_assign - 1)]
    in_block = jnp.arange(MOE_ROWS, dtype=jnp.int32)[None, :]
    slot = block_first[:, None] + in_block
    spare = n_assign + (jnp.arange(n_blocks, dtype=jnp.int32) % 2)[:, None] * MOE_ROWS + in_block
    slot_tok = jnp.where(valid, src % T, slot % T).reshape(-1)
    slot_dst = jnp.where(valid, src, spare).reshape(-1)
    slot_w = jnp.where(valid, wgt[src], 0.0).reshape(-1)
    n_used = (pend[-1:] // MOE_ROWS).astype(jnp.int32)
    return slot_tok, slot_dst, slot_w, block_e, n_used


def _layer(x, h, mods, mods_next, W, layer, ctx, rope_tabs):
    B, L, D = x.shape
    T = B * L
    z = _matmul([h.reshape(T, D)], W['w_in'], layer, (D,), F32)
    z3 = z.reshape(B, L, N_IN)

    ret, s_new = _retention(z3, W['log_decay'], ctx[0] if ctx else None, W['ret_gn_w'], layer, ctx is not None)
    sm = _spatial_gating(z3, W['sm_ln_w'], W['sm_w'], W['sm_bT'], layer)
    v_new = z3[:, :, OFF_AV:OFF_AV + D_KV]
    if ctx is None:
        q16, k16, k_new = _qk_prep(z3, W['q_norm_w'], W['k_norm_w'], layer, None)
        keys, vals = k16, v_new.astype(BF16)
        new_ctx = (s_new, k_new.reshape(B, L, KV_HEADS, HEAD_DIM), v_new.reshape(B, L, KV_HEADS, HEAD_DIM))
    else:
        q16, k16 = _qk_prep(z3, W['q_norm_w'], W['k_norm_w'], layer, rope_tabs)
        keys = jnp.concatenate([k16, ctx[1]], axis=1)
        vals = jnp.concatenate([v_new.astype(BF16), ctx[2]], axis=1)
        new_ctx = None
    att = _attention(q16, keys, vals)

    mix = _matmul([ret.reshape(T, D_RET), sm.reshape(T, D_SM), att.reshape(T, D_ATT)], W['w_out'], layer,
                  (D_RET, D_SM, D_ATT), F32)
    x1, h2, logits = _ln_block(x, [(mix, 0)], mods, 2, W['ln1_w'], W['ln1_b'], layer,
                               mods_h=mods, mod_idx=3, router=(W['router_w'], W['router_b']))

    slot_tok, slot_dst, slot_w, block_e, n_used = _route(logits.reshape(T, ROUTER_PAD))
    ffn = _experts(h2.reshape(T, D), slot_tok, slot_dst, block_e, n_used, slot_w, W['w_up'], W['w_down'], layer)
    ys = [(ffn, 0), (ffn, T // LN_ROWS)]
    if mods_next is None:
        (x2,) = _ln_block(x1, ys, mods, 5, W['ln2_w'], W['ln2_b'], layer)
        h_next = None
    else:
        x2, h_next = _ln_block(x1, ys, mods, 5, W['ln2_w'], W['ln2_b'], layer, mods_h=mods_next, mod_idx=0)
    return x2, h_next, new_ctx


def kernel(x_prompt, x_sample, state_ret, cache_k, cache_v, c, c_ctx, w_mod, b_mod, w_in, ret_log_decay, ret_gn_w, sm_ln_w, sm_w, sm_b, q_norm_w, k_norm_w, w_out, ln1_w, ln1_b, router_w_coarse, router_b_coarse, router_w_fine, router_b_fine, w_up, w_down, ln2_w, ln2_b):
    n_dec = c.shape[0]
    cond = jnp.zeros((16, D_MODEL), F32).at[:n_dec].set(c).at[n_dec].set(c_ctx)
    mods_all = _adaln_mods(cond, w_mod, b_mod)
    mods_lat = [mods_all[l, :n_dec].reshape(n_dec, N_MOD, D_MODEL) for l in range(DEPTH)]
    mods_ctx = [mods_all[l, n_dec:n_dec + 1].reshape(1, N_MOD, D_MODEL) for l in range(DEPTH)]

    n_r = N_GROUPS + N_EXPERTS
    router_w = jnp.zeros((DEPTH, D_MODEL, ROUTER_PAD), F32).at[:, :, :n_r].set(
        jnp.concatenate([router_w_coarse, router_w_fine], axis=-1))
    router_b = jnp.zeros((DEPTH, 1, ROUTER_PAD), F32).at[:, 0, :n_r].set(
        jnp.concatenate([router_b_coarse, router_b_fine], axis=-1))
    W = {
        'w_in': w_in.astype(BF16), 'w_out': w_out.astype(BF16),
        'w_up': w_up.astype(BF16), 'w_down': w_down.astype(BF16),
        'log_decay': ret_log_decay.reshape(-1), 'ret_gn_w': ret_gn_w,
        'sm_ln_w': sm_ln_w, 'sm_w': sm_w.astype(BF16), 'sm_bT': jnp.swapaxes(sm_b, 1, 2),
        'q_norm_w': q_norm_w, 'k_norm_w': k_norm_w,
        'ln1_w': ln1_w, 'ln1_b': ln1_b, 'ln2_w': ln2_w, 'ln2_b': ln2_b,
        'router_w': router_w, 'router_b': router_b,
    }
    rope_tabs = _rope_tables(x_sample.shape[1])
    past = cache_k.shape[2]
    ck16 = cache_k.astype(BF16).reshape(n_dec, DEPTH, past, D_KV)
    cv16 = cache_v.astype(BF16).reshape(n_dec, DEPTH, past, D_KV)

    y_p, y_s = x_prompt, x_sample
    h_p = _modulate(x_prompt, mods_ctx[0])
    h_s = _modulate(x_sample, mods_lat[0])
    states, ctx_k, ctx_v = [], [], []
    for l in range(DEPTH):
        nxt = l + 1 < DEPTH
        y_p, h_p, (s_l, k_l, v_l) = _layer(y_p, h_p, mods_ctx[l], mods_ctx[l + 1] if nxt else None, W, l, None, None)
        states.append(s_l)
        ctx_k.append(k_l)
        ctx_v.append(v_l)
        y_s, h_s, _ = _layer(y_s, h_s, mods_lat[l], mods_lat[l + 1] if nxt else None, W, l,
                             (state_ret, ck16[:, l], cv16[:, l]), rope_tabs)
    return (y_p, y_s, jnp.stack(states, axis=1), jnp.stack(ctx_k, axis=1), jnp.stack(ctx_v, axis=1))
```

```python
import functools

import numpy as np
import jax
import jax.numpy as jnp
from jax import lax
from jax.experimental import pallas as pl
from jax.experimental.pallas import tpu as pltpu

F32 = jnp.float32
BF16 = jnp.bfloat16

D_MODEL = 4096
DEPTH = 2
GRID_W = 64
HEAD_DIM = 128
H_RET = 8
D_RET = H_RET * HEAD_DIM
SM_GROUPS = 8
D_SM = 1024
H_ATT = 16
KV_HEADS = 4
Q_PER_KV = H_ATT // KV_HEADS
D_ATT = H_ATT * HEAD_DIM
D_KV = KV_HEADS * HEAD_DIM
CHUNK = 128
ROPE_PAIRS = 32
ROPE_THETA = 10000.0
N_GROUPS = 4
EXPERTS_PER_GROUP = 8
N_EXPERTS = N_GROUPS * EXPERTS_PER_GROUP
D_EXPERT = D_MODEL // 4
N_MOD = 6
N_IN = D_RET * 4 + D_SM * 2 + D_ATT + D_KV * 2
DEEPNORM_ALPHA = (2 * DEPTH) ** 0.25
LN_EPS = 1e-5
RMS_EPS = 1e-6
Q_PRESCALE = float(HEAD_DIM ** -0.5 * np.log2(np.e))

COL_RQ, COL_RK, COL_RV, COL_RG = 0, H_RET, 2 * H_RET, 3 * H_RET
COL_SU, COL_SV = 4, 5
COL_AQ = 3
COL_AK, COL_AV = 16, 17
OFF_AV = COL_AV * D_KV

LANES = 128
ROUTER_PAD = LANES
MIB = 1024 * 1024
MOE_ROWS = 256
TM = 1024
TN = 1024
LN_ROWS = 128
MOD_ROWS = 512


def _cparams(sem, vmem_mib):
    return pltpu.CompilerParams(dimension_semantics=sem, vmem_limit_bytes=vmem_mib * MIB)


def _sigmoid(x):
    return 1.0 / (1.0 + jnp.exp(-x))


def _silu(x):
    return x * _sigmoid(x)


def _mods_kernel(c_ref, w_ref, b_ref, o_ref):
    a = _silu(c_ref[...]).astype(BF16)
    o_ref[...] = jnp.dot(a, w_ref[...].astype(BF16), preferred_element_type=F32) + b_ref[...]


def _adaln_mods(cond, w_mod, b_mod):
    n = cond.shape[0]
    tn = 512
    n_out = N_MOD * D_MODEL
    return pl.pallas_call(
        _mods_kernel,
        out_shape=jax.ShapeDtypeStruct((DEPTH, n, n_out), F32),
        grid=(DEPTH, n_out // tn),
        in_specs=[
            pl.BlockSpec((n, D_MODEL), lambda l, j: (0, 0)),
            pl.BlockSpec((None, D_MODEL, tn), lambda l, j: (l, 0, j)),
            pl.BlockSpec((None, 1, tn), lambda l, j: (l, 0, j)),
        ],
        out_specs=pl.BlockSpec((None, n, tn), lambda l, j: (l, 0, j)),
        compiler_params=_cparams(("arbitrary", "arbitrary"), 40),
        name="adaln_mods",
    )(cond, w_mod, b_mod.reshape(DEPTH, 1, n_out))


def _modulate_kernel(x_ref, m_ref, o_ref):
    o_ref[...] = (x_ref[...] * (1.0 + m_ref[1:2, :]) + m_ref[0:1, :]).astype(BF16)


def _mods_index(bm):
    if bm == 1:
        return lambda b, i: (0, 0, 0)
    return lambda b, i: (b, 0, 0)


def _modulate(x, mods):
    B, L, D = x.shape
    tl = min(MOD_ROWS, L)
    return pl.pallas_call(
        _modulate_kernel,
        out_shape=jax.ShapeDtypeStruct((B, L, D), BF16),
        grid=(B, L // tl),
        in_specs=[
            pl.BlockSpec((None, tl, D), lambda b, i: (b, i, 0)),
            pl.BlockSpec((None, N_MOD, D), _mods_index(mods.shape[0])),
        ],
        out_specs=pl.BlockSpec((None, tl, D), lambda b, i: (b, i, 0)),
        compiler_params=_cparams(("arbitrary", "arbitrary"), 40),
        name="modulate",
    )(x, mods)


def _mm_kernel(*refs, n_a):
    o_ref = refs[2 * n_a]
    acc = None
    for a_ref, w_ref in zip(refs[:n_a], refs[n_a:2 * n_a]):
        d = jnp.dot(a_ref[...], w_ref[...], preferred_element_type=F32)
        acc = d if acc is None else acc + d
    o_ref[...] = acc.astype(o_ref.dtype)


def _matmul(a_list, w, layer, k_sizes, out_dtype):
    T = a_list[0].shape[0]
    N = w.shape[2]
    tm = min(TM, T)
    in_specs = [pl.BlockSpec((tm, k), lambda i, j: (i, 0)) for k in k_sizes]
    row = 0
    for k in k_sizes:
        assert row % k == 0
        in_specs.append(pl.BlockSpec((None, k, TN), functools.partial(lambda i, j, rb: (layer, rb, j), rb=row // k)))
        row += k
    return pl.pallas_call(
        functools.partial(_mm_kernel, n_a=len(a_list)),
        out_shape=jax.ShapeDtypeStruct((T, N), out_dtype),
        grid=(T // tm, N // TN),
        in_specs=in_specs,
        out_specs=pl.BlockSpec((tm, TN), lambda i, j: (i, j)),
        compiler_params=_cparams(("arbitrary", "arbitrary"), 56),
        name="matmul",
    )(*a_list, *([w] * len(a_list)))


def _normalize_rows(x):
    mu = jnp.mean(x, axis=-1, keepdims=True)
    xc = x - mu
    var = jnp.mean(xc * xc, axis=-1, keepdims=True)
    return xc * lax.rsqrt(var + LN_EPS)


RET_HEADS = 2
RET_UNROLL = 4


def _ret_consts(lg_f, lg_b):
    C = CHUNK
    ii = lax.broadcasted_iota(jnp.int32, (C, C), 0).astype(F32)
    jj = lax.broadcasted_iota(jnp.int32, (C, C), 1).astype(F32)
    diff = ii - jj
    mask = (jnp.where(diff >= 0, jnp.exp(lg_f * jnp.maximum(diff, 0.0)), 0.0)
            + jnp.where(diff <= 0, jnp.exp(lg_b * jnp.maximum(-diff, 0.0)), 0.0))
    pos = lax.broadcasted_iota(jnp.int32, (C, 1), 0).astype(F32)
    return dict(
        mask=mask,
        qdec_f=jnp.exp(lg_f * (pos + 1.0)), kdec_f=jnp.exp(lg_f * (C - 1.0 - pos)),
        qdec_b=jnp.exp(lg_b * (C - pos)), kdec_b=jnp.exp(lg_b * pos),
        cdec_f=jnp.exp(jnp.full((1, HEAD_DIM), lg_f * C, F32)),
        cdec_b=jnp.exp(jnp.full((1, HEAD_DIM), lg_b * C, F32)))


def _ret_kernel(ld_ref, q_ref, k_ref, v_ref, g_ref, s0_ref, gnw_ref, o_ref, sout_ref, oacc_ref, *,
                layer, n_chunks, has_ctx):
    hb = pl.program_id(1)
    C = CHUNK
    consts = []
    for j in range(RET_HEADS):
        h = hb * RET_HEADS + j
        consts.append(_ret_consts(ld_ref[layer * 2 * H_RET + h], ld_ref[layer * 2 * H_RET + H_RET + h]))
    k_scale = HEAD_DIM ** -0.5
    nt = (((1,), (1,)), ((), ()))

    def load_chunk(c, j):
        rows = pl.ds(pl.multiple_of(c * C, C), C)
        cols = slice(j * HEAD_DIM, (j + 1) * HEAD_DIM)
        q = q_ref[rows, cols].astype(BF16)
        kf = k_ref[rows, cols] * k_scale
        v = v_ref[rows, cols].astype(BF16)
        return rows, cols, q, kf, v

    def state_update(s, kf, kdec, cdec, v):
        kd = (kf * kdec).astype(BF16)
        return s * cdec + jnp.dot(kd.T, v, preferred_element_type=F32)

    def fwd(c, states):
        out = []
        for j, s in enumerate(states):
            cj = consts[j]
            rows, cols, q, kf, v = load_chunk(c, j)
            qk = lax.dot_general(q, kf.astype(BF16), nt, preferred_element_type=F32)
            a = (qk * cj['mask']).astype(BF16)
            o = (jnp.dot(a, v, preferred_element_type=F32)
                 + jnp.dot(q, s.astype(BF16), preferred_element_type=F32) * cj['qdec_f'])
            oacc_ref[rows, cols] = o
            out.append(state_update(s, kf, cj['kdec_f'], cj['cdec_f'], v))
        return tuple(out)

    def bwd(t, states):
        c = n_chunks - 1 - t
        out = []
        for j, s in enumerate(states):
            cj = consts[j]
            rows, cols, q, kf, v = load_chunk(c, j)
            o = oacc_ref[rows, cols] + jnp.dot(q, s.astype(BF16), preferred_element_type=F32) * cj['qdec_b']
            y = _normalize_rows(o) * gnw_ref[:, cols] * _silu(g_ref[rows, cols])
            o_ref[rows, cols] = y.astype(o_ref.dtype)
            out.append(state_update(s, kf, cj['kdec_b'], cj['cdec_b'], v))
        return tuple(out)

    if has_ctx:
        s0_f = tuple(s0_ref[0, j] for j in range(RET_HEADS))
        s0_b = tuple(s0_ref[1, j] for j in range(RET_HEADS))
    else:
        s0_f = tuple(jnp.zeros((HEAD_DIM, HEAD_DIM), F32) for _ in range(RET_HEADS))
        s0_b = s0_f
    unroll = min(RET_UNROLL, n_chunks)
    s_f = lax.fori_loop(0, n_chunks, fwd, s0_f, unroll=unroll)
    s_b = lax.fori_loop(0, n_chunks, bwd, s0_b, unroll=unroll)
    for j in range(RET_HEADS):
        sout_ref[0, j] = s_f[j]
        sout_ref[1, j] = s_b[j]


def _retention(z3, log_decay_flat, state_ret, gn_w, layer, has_ctx):
    B, L, _ = z3.shape
    hp = RET_HEADS
    width = hp * HEAD_DIM
    col = lambda off: pl.BlockSpec((None, L, width), lambda b, h, ld: (b, 0, off // hp + h))
    state_block = (None, None, 2, hp, HEAD_DIM, HEAD_DIM)
    if has_ctx:
        s0 = state_ret
        s0_spec = pl.BlockSpec(state_block, lambda b, h, ld: (b, layer, 0, h, 0, 0))
    else:
        s0 = jnp.zeros((1, 1, 2, hp, HEAD_DIM, HEAD_DIM), F32)
        s0_spec = pl.BlockSpec(state_block, lambda b, h, ld: (0, 0, 0, 0, 0, 0))
    grid_spec = pltpu.PrefetchScalarGridSpec(
        num_scalar_prefetch=1,
        grid=(B, H_RET // hp),
        in_specs=[col(COL_RQ), col(COL_RK), col(COL_RV), col(COL_RG), s0_spec,
                  pl.BlockSpec((None, 1, width), lambda b, h, ld: (layer, 0, h))],
        out_specs=[pl.BlockSpec((None, L, width), lambda b, h, ld: (b, 0, h)),
                   pl.BlockSpec((None, 2, hp, HEAD_DIM, HEAD_DIM), lambda b, h, ld: (b, 0, h, 0, 0))],
        scratch_shapes=[pltpu.VMEM((L, width), F32)],
    )
    return pl.pallas_call(
        functools.partial(_ret_kernel, layer=layer, n_chunks=L // CHUNK, has_ctx=has_ctx),
        out_shape=[jax.ShapeDtypeStruct((B, L, D_RET), BF16),
                   jax.ShapeDtypeStruct((B, 2, H_RET, HEAD_DIM, HEAD_DIM), F32)],
        grid_spec=grid_spec,
        compiler_params=_cparams(("arbitrary", "arbitrary"), 52),
        name="retention",
    )(log_decay_flat, z3, z3, z3, z3, s0, gn_w.reshape(DEPTH, 1, D_RET))


def _sg_kernel(u_ref, v_ref, lnw_ref, ws_ref, bs_ref, o_ref, *, n_sub):
    for s in range(n_sub):
        rows = slice(s * CHUNK, (s + 1) * CHUNK)
        u = jax.nn.gelu(u_ref[rows, :])
        vn = (_normalize_rows(jax.nn.gelu(v_ref[rows, :])) * lnw_ref[...]).astype(BF16)
        for g in range(SM_GROUPS):
            cols = slice(g * LANES, (g + 1) * LANES)
            sg = jnp.dot(ws_ref[g], vn[:, cols], preferred_element_type=F32) + bs_ref[:, g:g + 1]
            o_ref[rows, cols] = (u[:, cols] * sg).astype(o_ref.dtype)


def _spatial_gating(z3, sm_ln_w, sm_w16, sm_bT, layer):
    B, L, _ = z3.shape
    ts = 2 * CHUNK
    return pl.pallas_call(
        functools.partial(_sg_kernel, n_sub=ts // CHUNK),
        out_shape=jax.ShapeDtypeStruct((B, L, D_SM), BF16),
        grid=(B, L // ts),
        in_specs=[
            pl.BlockSpec((None, ts, D_SM), lambda b, i: (b, i, COL_SU)),
            pl.BlockSpec((None, ts, D_SM), lambda b, i: (b, i, COL_SV)),
            pl.BlockSpec((None, 1, D_SM), lambda b, i: (layer, 0, 0)),
            pl.BlockSpec((None, SM_GROUPS, CHUNK, CHUNK), lambda b, i: (layer, 0, 0, 0)),
            pl.BlockSpec((None, CHUNK, SM_GROUPS), lambda b, i: (layer, 0, 0)),
        ],
        out_specs=pl.BlockSpec((None, ts, D_SM), lambda b, i: (b, i, 0)),
        compiler_params=_cparams(("arbitrary", "arbitrary"), 40),
        name="spatial_gating",
    )(z3, z3, sm_ln_w.reshape(DEPTH, 1, D_SM), sm_w16, sm_bT)


def _rms_rows(x, w):
    return x * lax.rsqrt(jnp.mean(x * x, axis=-1, keepdims=True) + RMS_EPS) * w


def _qk_kernel(*refs, rope):
    if rope:
        q_ref, k_ref, qw_ref, kw_ref, cos_ref, sin_ref, qo_ref, ko_ref = refs
        cos = cos_ref[...]
        sin = sin_ref[...]
        lane = lax.broadcasted_iota(jnp.int32, cos.shape, 1)
        first_half = (lane % (2 * ROPE_PAIRS)) < ROPE_PAIRS
    else:
        q_ref, k_ref, qw_ref, kw_ref, qo_ref, ko_ref, kf_ref = refs

    def rot(x):
        if not rope:
            return x
        partner = jnp.where(first_half, pltpu.roll(x, LANES - ROPE_PAIRS, axis=1), pltpu.roll(x, ROPE_PAIRS, axis=1))
        return x * cos + partner * sin

    for hd in range(H_ATT):
        cols = slice(hd * HEAD_DIM, (hd + 1) * HEAD_DIM)
        qo_ref[:, cols] = (rot(_rms_rows(q_ref[:, cols], qw_ref[...])) * Q_PRESCALE).astype(qo_ref.dtype)
    for hd in range(KV_HEADS):
        cols = slice(hd * HEAD_DIM, (hd + 1) * HEAD_DIM)
        kn = _rms_rows(k_ref[:, cols], kw_ref[...])
        if not rope:
            kf_ref[:, cols] = kn
        ko_ref[:, cols] = rot(kn).astype(ko_ref.dtype)


def _qk_prep(z3, q_norm_w, k_norm_w, layer, rope_tabs):
    B, L, _ = z3.shape
    tl = 256
    rope = rope_tabs is not None
    in_specs = [
        pl.BlockSpec((None, tl, D_ATT), lambda b, i: (b, i, COL_AQ)),
        pl.BlockSpec((None, tl, D_KV), lambda b, i: (b, i, COL_AK)),
        pl.BlockSpec((None, 1, HEAD_DIM), lambda b, i: (layer, 0, 0)),
        pl.BlockSpec((None, 1, HEAD_DIM), lambda b, i: (layer, 0, 0)),
    ]
    args = [z3, z3, q_norm_w.reshape(DEPTH, 1, HEAD_DIM), k_norm_w.reshape(DEPTH, 1, HEAD_DIM)]
    out_shape = [jax.ShapeDtypeStruct((B, L, D_ATT), BF16), jax.ShapeDtypeStruct((B, L, D_KV), BF16)]
    out_specs = [pl.BlockSpec((None, tl, D_ATT), lambda b, i: (b, i, 0)),
                 pl.BlockSpec((None, tl, D_KV), lambda b, i: (b, i, 0))]
    if rope:
        in_specs += [pl.BlockSpec((tl, HEAD_DIM), lambda b, i: (i, 0))] * 2
        args += list(rope_tabs)
    else:
        out_shape.append(jax.ShapeDtypeStruct((B, L, D_KV), F32))
        out_specs.append(pl.BlockSpec((None, tl, D_KV), lambda b, i: (b, i, 0)))
    return pl.pallas_call(
        functools.partial(_qk_kernel, rope=rope),
        out_shape=out_shape,
        grid=(B, L // tl),
        in_specs=in_specs,
        out_specs=out_specs,
        compiler_params=_cparams(("arbitrary", "arbitrary"), 40),
        name="qk_prep",
    )(*args)


def _rope_tables(n_tokens):
    rows = n_tokens // GRID_W
    row = jnp.broadcast_to(jnp.arange(rows, dtype=F32)[:, None], (rows, GRID_W)).reshape(-1)
    col = jnp.broadcast_to(jnp.arange(GRID_W, dtype=F32)[None, :], (rows, GRID_W)).reshape(-1)
    inv = ROPE_THETA ** (-2.0 * jnp.arange(ROPE_PAIRS, dtype=F32) / (2 * ROPE_PAIRS))
    ang_r = row[:, None] * inv
    ang_c = col[:, None] * inv
    cos = jnp.concatenate([jnp.cos(ang_r), jnp.cos(ang_r), jnp.cos(ang_c), jnp.cos(ang_c)], axis=1)
    sin = jnp.concatenate([-jnp.sin(ang_r), jnp.sin(ang_r), -jnp.sin(ang_c), jnp.sin(ang_c)], axis=1)
    return cos, sin


def _att_kernel(q_ref, k_ref, v1_ref, o_ref):
    k = k_ref[...]
    v1 = v1_ref[...]
    nt = (((1,), (1,)), ((), ()))
    for g in range(Q_PER_KV):
        cols = slice(g * HEAD_DIM, (g + 1) * HEAD_DIM)
        for r0 in range(0, q_ref.shape[0], ATT_UNIT_ROWS):
            rows = slice(r0, r0 + ATT_UNIT_ROWS)
            s = lax.dot_general(q_ref[rows, cols], k, nt, preferred_element_type=F32)
            p = jnp.exp2(s - jnp.max(s, axis=-1, keepdims=True)).astype(BF16)
            ol = jnp.dot(p, v1, preferred_element_type=F32)
            o_ref[rows, cols] = (ol[:, :HEAD_DIM] / ol[:, HEAD_DIM:]).astype(o_ref.dtype)


ATT_UNIT_ROWS = 256
ATT_UNITS_PER_HEAD = 4


def _attention(q, keys, vals):
    B, L, _ = q.shape
    Lk = keys.shape[1]
    tq = min(ATT_UNITS_PER_HEAD * ATT_UNIT_ROWS, L)
    blk = Q_PER_KV * HEAD_DIM
    v4 = vals.reshape(B, Lk, KV_HEADS, HEAD_DIM)
    vals1 = jnp.concatenate([v4, jnp.ones_like(v4)], axis=-1).reshape(B, Lk, 2 * D_KV)
    return pl.pallas_call(
        _att_kernel,
        out_shape=jax.ShapeDtypeStruct((B, L, D_ATT), BF16),
        grid=(B, KV_HEADS, L // tq),
        in_specs=[
            pl.BlockSpec((None, tq, blk), lambda b, h, i: (b, i, h)),
            pl.BlockSpec((None, Lk, HEAD_DIM), lambda b, h, i: (b, 0, h)),
            pl.BlockSpec((None, Lk, 2 * HEAD_DIM), lambda b, h, i: (b, 0, h)),
        ],
        out_specs=pl.BlockSpec((None, tq, blk), lambda b, h, i: (b, i, h)),
        compiler_params=_cparams(("arbitrary", "arbitrary", "arbitrary"), 48),
        name="attention",
    )(q, keys, vals1)


def _ln_kernel(*refs, n_y, gate_idx, mod_idx, with_router):
    x_ref = refs[0]
    y_refs = refs[1:1 + n_y]
    mg_ref, lnw_ref, lnb_ref = refs[1 + n_y:4 + n_y]
    rest = list(refs[4 + n_y:])
    y = y_refs[0][...]
    for r in y_refs[1:]:
        y = y + r[...]
    t = DEEPNORM_ALPHA * x_ref[...] + mg_ref[gate_idx:gate_idx + 1, :] * y
    xn = _normalize_rows(t) * lnw_ref[...] + lnb_ref[...]
    if mod_idx is None:
        (xo_ref,) = rest
        xo_ref[...] = xn
        return
    mh_ref = rest.pop(0)
    h = xn * (1.0 + mh_ref[mod_idx + 1:mod_idx + 2, :]) + mh_ref[mod_idx:mod_idx + 1, :]
    if with_router:
        wr_ref, br_ref, xo_ref, ho_ref, lg_ref = rest
        lg_ref[...] = jnp.dot(h, wr_ref[...], preferred_element_type=F32,
                              precision=lax.Precision.HIGHEST) + br_ref[...]
    else:
        xo_ref, ho_ref = rest
    xo_ref[...] = xn
    ho_ref[...] = h.astype(ho_ref.dtype)


def _ln_block(x, ys, mods_gate, gate_idx, ln_w, ln_b, layer, mods_h=None, mod_idx=None, router=None):
    B, L, D = x.shape
    tl = LN_ROWS
    row_spec = pl.BlockSpec((None, tl, D), lambda b, i: (b, i, 0))
    in_specs = [row_spec]
    args = [x]
    for arr, first in ys:
        in_specs.append(pl.BlockSpec((tl, D), functools.partial(lambda b, i, first: (first + b * (L // tl) + i, 0),
                                                                first=first)))
        args.append(arr)
    vec_spec = pl.BlockSpec((None, 1, D), lambda b, i: (layer, 0, 0))
    in_specs += [pl.BlockSpec((None, N_MOD, D), _mods_index(mods_gate.shape[0])), vec_spec, vec_spec]
    args += [mods_gate, ln_w.reshape(DEPTH, 1, D), ln_b.reshape(DEPTH, 1, D)]
    out_shape = [jax.ShapeDtypeStruct((B, L, D), F32)]
    out_specs = [row_spec]
    if mod_idx is not None:
        in_specs.append(pl.BlockSpec((None, N_MOD, D), _mods_index(mods_h.shape[0])))
        args.append(mods_h)
        if router is not None:
            w_r, b_r = router
            in_specs += [pl.BlockSpec((None, D, ROUTER_PAD), lambda b, i: (layer, 0, 0)),
                         pl.BlockSpec((None, 1, ROUTER_PAD), lambda b, i: (layer, 0, 0))]
            args += [w_r, b_r]
            out_shape += [jax.ShapeDtypeStruct((B, L, D), F32), jax.ShapeDtypeStruct((B, L, ROUTER_PAD), F32)]
            out_specs += [row_spec, pl.BlockSpec((None, tl, ROUTER_PAD), lambda b, i: (b, i, 0))]
        else:
            out_shape.append(jax.ShapeDtypeStruct((B, L, D), BF16))
            out_specs.append(row_spec)
    return pl.pallas_call(
        functools.partial(_ln_kernel, n_y=len(ys), gate_idx=gate_idx, mod_idx=mod_idx,
                          with_router=router is not None),
        out_shape=out_shape,
        grid=(B, L // tl),
        in_specs=in_specs,
        out_specs=out_specs,
        compiler_params=_cparams(("arbitrary", "arbitrary"), 48),
        name="layernorm",
    )(*args)


MOE_BUFS = 3


def _up_kernel(be_ref, nu_ref, tok_ref, x_hbm, wg_ref, wu_ref, o_ref, xbuf, sem):
    b = pl.program_id(0)
    n_used = nu_ref[0]

    def row_copy(block, r):
        to = block % MOE_BUFS
        t = tok_ref[block * MOE_ROWS + r]
        return pltpu.make_async_copy(x_hbm.at[pl.ds(t, 1), :], xbuf.at[to, pl.ds(r, 1), :], sem.at[to])

    def issue_loop(block):
        def body(r, carry):
            row_copy(block, r).start()
            return carry

        lax.fori_loop(0, MOE_ROWS, body, 0)

    def wait_block(block):
        s = block % MOE_BUFS
        pltpu.make_async_copy(x_hbm.at[pl.ds(0, MOE_ROWS), :], xbuf.at[s], sem.at[s]).wait()

    @pl.when(b == 0)
    def _():
        issue_loop(0)

        @pl.when(1 < n_used)
        def _():
            issue_loop(1)

        wait_block(0)

    def compute(prefetch):
        x = xbuf[b % MOE_BUFS].astype(BF16)
        if prefetch:
            for r in range(MOE_ROWS):
                row_copy(b + 2, r).start()
        hg = jnp.dot(x, wg_ref[...], preferred_element_type=F32)
        hu = jnp.dot(x, wu_ref[...], preferred_element_type=F32)
        o_ref[...] = (_silu(hg) * hu).astype(o_ref.dtype)

    @pl.when(b + 2 < n_used)
    def _():
        compute(True)

    @pl.when((b < n_used) & (b + 2 >= n_used))
    def _():
        compute(False)

    @pl.when(b >= n_used)
    def _():
        o_ref[...] = jnp.zeros_like(o_ref)

    @pl.when(b + 1 < n_used)
    def _():
        wait_block(b + 1)


def _down_kernel(be_ref, nu_ref, dst_ref, a_ref, wd_ref, sw_ref, out_hbm, y0, y1, y2, sem, *, n_rows):
    b = pl.program_id(0)
    n_used = nu_ref[0]
    bufs = (y0, y1, y2)

    def row_copy(k, block, r):
        d = dst_ref[block * MOE_ROWS + r]
        return pltpu.make_async_copy(bufs[k].at[pl.ds(r, 1), :], out_hbm.at[pl.ds(d, 1), :], sem.at[k])

    def product():
        return jnp.dot(a_ref[...], wd_ref[...], preferred_element_type=F32) * sw_ref[...]

    @pl.when(b == 0)
    def _():
        for s in range(2):
            bufs[s][...] = jnp.zeros_like(bufs[s])
            fill = pltpu.make_async_copy(bufs[s], out_hbm.at[pl.ds(n_rows + s * MOE_ROWS, MOE_ROWS), :], sem.at[s])
            fill.start()
            fill.wait()
        y0[...] = product()

    for k in range(MOE_BUFS):
        prev = (k - 1) % MOE_BUFS

        @pl.when((b % MOE_BUFS == k) & (b >= 1) & (b < n_used))
        def _():
            for r in range(MOE_ROWS):
                row_copy(prev, b - 1, r).start()
            bufs[k][...] = product()

        @pl.when((b % MOE_BUFS == k) & (b == n_used))
        def _():
            def body(r, carry):
                row_copy(prev, b - 1, r).start()
                return carry

            lax.fori_loop(0, MOE_ROWS, body, 0)

    for k in range(MOE_BUFS):
        @pl.when(((b + 1) % MOE_BUFS == k) & (b >= 2) & (b - 2 < n_used))
        def _():
            pltpu.make_async_copy(bufs[k], out_hbm.at[pl.ds(0, MOE_ROWS), :], sem.at[k]).wait()


def _experts(h2, slot_tok, slot_dst, block_e, n_used, slot_w, w_up16, w_down16, layer):
    T = h2.shape[0]
    n_slots = slot_tok.shape[0]
    nb = n_slots // MOE_ROWS
    up_spec = pltpu.PrefetchScalarGridSpec(
        num_scalar_prefetch=3,
        grid=(nb,),
        in_specs=[
            pl.BlockSpec(memory_space=pl.ANY),
            pl.BlockSpec((None, None, D_MODEL, D_EXPERT), lambda b, be, nu, tok: (layer, be[b], 0, 0)),
            pl.BlockSpec((None, None, D_MODEL, D_EXPERT), lambda b, be, nu, tok: (layer, be[b], 0, 1)),
        ],
        out_specs=pl.BlockSpec((MOE_ROWS, D_EXPERT), lambda b, be, nu, tok: (b, 0)),
        scratch_shapes=[pltpu.VMEM((MOE_BUFS, MOE_ROWS, D_MODEL), F32), pltpu.SemaphoreType.DMA((MOE_BUFS,))],
    )
    act = pl.pallas_call(
        _up_kernel,
        out_shape=jax.ShapeDtypeStruct((n_slots, D_EXPERT), BF16),
        grid_spec=up_spec,
        compiler_params=_cparams(("arbitrary",), 56),
        name="expert_up",
    )(block_e, n_used, slot_tok, h2, w_up16, w_up16)
    down_spec = pltpu.PrefetchScalarGridSpec(
        num_scalar_prefetch=3,
        grid=(nb,),
        in_specs=[
            pl.BlockSpec((MOE_ROWS, D_EXPERT), lambda b, be, nu, dst: (b, 0)),
            pl.BlockSpec((None, None, D_EXPERT, D_MODEL), lambda b, be, nu, dst: (layer, be[b], 0, 0)),
            pl.BlockSpec((MOE_ROWS, 1), lambda b, be, nu, dst: (b, 0)),
        ],
        out_specs=pl.BlockSpec(memory_space=pl.ANY),
        scratch_shapes=[pltpu.VMEM((MOE_ROWS, D_MODEL), F32)] * MOE_BUFS + [pltpu.SemaphoreType.DMA((MOE_BUFS,))],
    )
    return pl.pallas_call(
        functools.partial(_down_kernel, n_rows=2 * T),
        out_shape=jax.ShapeDtypeStruct((2 * T + 2 * MOE_ROWS, D_MODEL), F32),
        grid_spec=down_spec,
        compiler_params=_cparams(("arbitrary",), 48),
        name="expert_down",
    )(block_e, n_used, slot_dst, act, w_down16, slot_w.reshape(n_slots, 1))


def _route(logits):
    T = logits.shape[0]
    lc = logits[:, :N_GROUPS]
    lf = logits[:, N_GROUPS:N_GROUPS + N_EXPERTS].reshape(T, N_GROUPS, EXPERTS_PER_GROUP)
    grp = jnp.argmax(lc, axis=-1).astype(jnp.int32)
    p_grp = jnp.take_along_axis(jax.nn.softmax(lc, axis=-1), grp[:, None], axis=-1)
    lfg = jnp.take_along_axis(lf, grp[:, None, None], axis=1)[:, 0]
    top_v, top_i = lax.top_k(lfg, 2)
    gate = jax.nn.softmax(top_v, axis=-1) * p_grp
    eid = (grp[:, None] * EXPERTS_PER_GROUP + top_i.astype(jnp.int32)).T.reshape(-1)
    wgt = gate.T.reshape(-1)
    n_assign = 2 * T
    experts = jnp.arange(N_EXPERTS, dtype=jnp.int32)
    counts = jnp.sum(eid[:, None] == experts[None, :], axis=0, dtype=jnp.int32)
    start = jnp.cumsum(counts) - counts
    pcounts = (counts + MOE_ROWS - 1) // MOE_ROWS * MOE_ROWS
    pend = jnp.cumsum(pcounts)
    pstart = pend - pcounts
    order = jnp.argsort(eid).astype(jnp.int32)
    n_blocks = n_assign // MOE_ROWS + N_EXPERTS + 2
    block_first = jnp.arange(n_blocks, dtype=jnp.int32) * MOE_ROWS
    block_e = jnp.minimum(jnp.sum(pend[None, :] <= block_first[:, None], axis=1, dtype=jnp.int32), N_EXPERTS - 1)
    r = (block_first - pstart[block_e])[:, None] + jnp.arange(MOE_ROWS, dtype=jnp.int32)[None, :]
    valid = r < counts[block_e][:, None]
    src = order[jnp.clip(start[block_e][:, None] + r, 0, n_assign - 1)]
    in_block = jnp.arange(MOE_ROWS, dtype=jnp.int32)[None, :]
    slot = block_first[:, None] + in_block
    spare = n_assign + (jnp.arange(n_blocks, dtype=jnp.int32) % 2)[:, None] * MOE_ROWS + in_block
    slot_tok = jnp.where(valid, src % T, slot % T).reshape(-1)
    slot_dst = jnp.where(valid, src, spare).reshape(-1)
    slot_w = jnp.where(valid, wgt[src], 0.0).reshape(-1)
    n_used = (pend[-1:] // MOE_ROWS).astype(jnp.int32)
    return slot_tok, slot_dst, slot_w, block_e, n_used


def _layer(x, h, mods, mods_next, W, layer, ctx, rope_tabs):
    B, L, D = x.shape
    T = B * L
    z = _matmul([h.reshape(T, D)], W['w_in'], layer, (D,), F32)
    z3 = z.reshape(B, L, N_IN)

    ret, s_new = _retention(z3, W['log_decay'], ctx[0] if ctx else None, W['ret_gn_w'], layer, ctx is not None)
    sm = _spatial_gating(z3, W['sm_ln_w'], W['sm_w'], W['sm_bT'], layer)
    v_new = z3[:, :, OFF_AV:OFF_AV + D_KV]
    if ctx is None:
        q16, k16, k_new = _qk_prep(z3, W['q_norm_w'], W['k_norm_w'], layer, None)
        keys, vals = k16, v_new.astype(BF16)
        new_ctx = (s_new, k_new.reshape(B, L, KV_HEADS, HEAD_DIM), v_new.reshape(B, L, KV_HEADS, HEAD_DIM))
    else:
        q16, k16 = _qk_prep(z3, W['q_norm_w'], W['k_norm_w'], layer, rope_tabs)
        keys = jnp.concatenate([k16, ctx[1]], axis=1)
        vals = jnp.concatenate([v_new.astype(BF16), ctx[2]], axis=1)
        new_ctx = None
    att = _attention(q16, keys, vals)

    mix = _matmul([ret.reshape(T, D_RET), sm.reshape(T, D_SM), att.reshape(T, D_ATT)], W['w_out'], layer,
                  (D_RET, D_SM, D_ATT), F32)
    x1, h2, logits = _ln_block(x, [(mix, 0)], mods, 2, W['ln1_w'], W['ln1_b'], layer,
                               mods_h=mods, mod_idx=3, router=(W['router_w'], W['router_b']))

    slot_tok, slot_dst, slot_w, block_e, n_used = _route(logits.reshape(T, ROUTER_PAD))
    ffn = _experts(h2.reshape(T, D), slot_tok, slot_dst, block_e, n_used, slot_w, W['w_up'], W['w_down'], layer)
    ys = [(ffn, 0), (ffn, T // LN_ROWS)]
    if mods_next is None:
        (x2,) = _ln_block(x1, ys, mods, 5, W['ln2_w'], W['ln2_b'], layer)
        h_next = None
    else:
        x2, h_next = _ln_block(x1, ys, mods, 5, W['ln2_w'], W['ln2_b'], layer, mods_h=mods_next, mod_idx=0)
    return x2, h_next, new_ctx


def kernel(x_prompt, x_sample, state_ret, cache_k, cache_v, c, c_ctx, w_mod, b_mod, w_in, ret_log_decay, ret_gn_w, sm_ln_w, sm_w, sm_b, q_norm_w, k_norm_w, w_out, ln1_w, ln1_b, router_w_coarse, router_b_coarse, router_w_fine, router_b_fine, w_up, w_down, ln2_w, ln2_b):
    n_dec = c.shape[0]
    cond = jnp.zeros((16, D_MODEL), F32).at[:n_dec].set(c).at[n_dec].set(c_ctx)
    mods_all = _adaln_mods(cond, w_mod, b_mod)
    mods_lat = [mods_all[l, :n_dec].reshape(n_dec, N_MOD, D_MODEL) for l in range(DEPTH)]
    mods_ctx = [mods_all[l, n_dec:n_dec + 1].reshape(1, N_MOD, D_MODEL) for l in range(DEPTH)]

    n_r = N_GROUPS + N_EXPERTS
    router_w = jnp.zeros((DEPTH, D_MODEL, ROUTER_PAD), F32).at[:, :, :n_r].set(
        jnp.concatenate([router_w_coarse, router_w_fine], axis=-1))
    router_b = jnp.zeros((DEPTH, 1, ROUTER_PAD), F32).at[:, 0, :n_r].set(
        jnp.concatenate([router_b_coarse, router_b_fine], axis=-1))
    W = {
        'w_in': w_in.astype(BF16), 'w_out': w_out.astype(BF16),
        'w_up': w_up.astype(BF16), 'w_down': w_down.astype(BF16),
        'log_decay': ret_log_decay.reshape(-1), 'ret_gn_w': ret_gn_w,
        'sm_ln_w': sm_ln_w, 'sm_w': sm_w.astype(BF16), 'sm_bT': jnp.swapaxes(sm_b, 1, 2),
        'q_norm_w': q_norm_w, 'k_norm_w': k_norm_w,
        'ln1_w': ln1_w, 'ln1_b': ln1_b, 'ln2_w': ln2_w, 'ln2_b': ln2_b,
        'router_w': router_w, 'router_b': router_b,
    }
    rope_tabs = _rope_tables(x_sample.shape[1])
    past = cache_k.shape[2]
    ck16 = cache_k.astype(BF16).reshape(n_dec, DEPTH, past, D_KV)
    cv16 = cache_v.astype(BF16).reshape(n_dec, DEPTH, past, D_KV)

    y_p, y_s = x_prompt, x_sample
    h_p = _modulate(x_prompt, mods_ctx[0])
    h_s = _modulate(x_sample, mods_lat[0])
    states, ctx_k, ctx_v = [], [], []
    for l in range(DEPTH):
        nxt = l + 1 < DEPTH
        y_p, h_p, (s_l, k_l, v_l) = _layer(y_p, h_p, mods_ctx[l], mods_ctx[l + 1] if nxt else None, W, l, None, None)
        states.append(s_l)
        ctx_k.append(k_l)
        ctx_v.append(v_l)
        y_s, h_s, _ = _layer(y_s, h_s, mods_lat[l], mods_lat[l + 1] if nxt else None, W, l,
                             (state_ret, ck16[:, l], cv16[:, l]), rope_tabs)
    return (y_p, y_s, jnp.stack(states, axis=1), jnp.stack(ctx_k, axis=1), jnp.stack(ctx_v, axis=1))
```

```python
import functools

import numpy as np
import jax
import jax.numpy as jnp
from jax import lax
from jax.experimental import pallas as pl
from jax.experimental.pallas import tpu as pltpu

F32 = jnp.float32
BF16 = jnp.bfloat16

D_MODEL = 4096
DEPTH = 2
GRID_W = 64
HEAD_DIM = 128
H_RET = 8
D_RET = H_RET * HEAD_DIM
SM_GROUPS = 8
D_SM = 1024
H_ATT = 16
KV_HEADS = 4
Q_PER_KV = H_ATT // KV_HEADS
D_ATT = H_ATT * HEAD_DIM
D_KV = KV_HEADS * HEAD_DIM
CHUNK = 128
ROPE_PAIRS = 32
ROPE_THETA = 10000.0
N_GROUPS = 4
EXPERTS_PER_GROUP = 8
N_EXPERTS = N_GROUPS * EXPERTS_PER_GROUP
D_EXPERT = D_MODEL // 4
N_MOD = 6
N_IN = D_RET * 4 + D_SM * 2 + D_ATT + D_KV * 2
DEEPNORM_ALPHA = (2 * DEPTH) ** 0.25
LN_EPS = 1e-5
RMS_EPS = 1e-6
Q_PRESCALE = float(HEAD_DIM ** -0.5 * np.log2(np.e))

COL_RQ, COL_RK, COL_RV, COL_RG = 0, H_RET, 2 * H_RET, 3 * H_RET
COL_SU, COL_SV = 4, 5
COL_AQ = 3
COL_AK, COL_AV = 16, 17
OFF_AV = COL_AV * D_KV

LANES = 128
ROUTER_PAD = LANES
MIB = 1024 * 1024
MOE_ROWS = 256
TM = 1024
TN = 1024
LN_ROWS = 128
MOD_ROWS = 512


def _cparams(sem, vmem_mib):
    return pltpu.CompilerParams(dimension_semantics=sem, vmem_limit_bytes=vmem_mib * MIB)


def _sigmoid(x):
    return 1.0 / (1.0 + jnp.exp(-x))


def _silu(x):
    return x * _sigmoid(x)


def _mods_kernel(c_ref, w_ref, b_ref, o_ref):
    a = _silu(c_ref[...]).astype(BF16)
    o_ref[...] = jnp.dot(a, w_ref[...].astype(BF16), preferred_element_type=F32) + b_ref[...]


def _adaln_mods(cond, w_mod, b_mod):
    n = cond.shape[0]
    tn = 512
    n_out = N_MOD * D_MODEL
    return pl.pallas_call(
        _mods_kernel,
        out_shape=jax.ShapeDtypeStruct((DEPTH, n, n_out), F32),
        grid=(DEPTH, n_out // tn),
        in_specs=[
            pl.BlockSpec((n, D_MODEL), lambda l, j: (0, 0)),
            pl.BlockSpec((None, D_MODEL, tn), lambda l, j: (l, 0, j)),
            pl.BlockSpec((None, 1, tn), lambda l, j: (l, 0, j)),
        ],
        out_specs=pl.BlockSpec((None, n, tn), lambda l, j: (l, 0, j)),
        compiler_params=_cparams(("arbitrary", "arbitrary"), 40),
        name="adaln_mods",
    )(cond, w_mod, b_mod.reshape(DEPTH, 1, n_out))


def _modulate_kernel(x_ref, m_ref, o_ref):
    o_ref[...] = (x_ref[...] * (1.0 + m_ref[1:2, :]) + m_ref[0:1, :]).astype(BF16)


def _mods_index(bm):
    if bm == 1:
        return lambda b, i: (0, 0, 0)
    return lambda b, i: (b, 0, 0)


def _modulate(x, mods):
    B, L, D = x.shape
    tl = min(MOD_ROWS, L)
    return pl.pallas_call(
        _modulate_kernel,
        out_shape=jax.ShapeDtypeStruct((B, L, D), BF16),
        grid=(B, L // tl),
        in_specs=[
            pl.BlockSpec((None, tl, D), lambda b, i: (b, i, 0)),
            pl.BlockSpec((None, N_MOD, D), _mods_index(mods.shape[0])),
        ],
        out_specs=pl.BlockSpec((None, tl, D), lambda b, i: (b, i, 0)),
        compiler_params=_cparams(("arbitrary", "arbitrary"), 40),
        name="modulate",
    )(x, mods)


def _mm_kernel(*refs, n_a):
    o_ref = refs[2 * n_a]
    acc = None
    for a_ref, w_ref in zip(refs[:n_a], refs[n_a:2 * n_a]):
        d = jnp.dot(a_ref[...], w_ref[...], preferred_element_type=F32)
        acc = d if acc is None else acc + d
    o_ref[...] = acc.astype(o_ref.dtype)


def _matmul(a_list, w, layer, k_sizes, out_dtype):
    T = a_list[0].shape[0]
    N = w.shape[2]
    tm = min(TM, T)
    in_specs = [pl.BlockSpec((tm, k), lambda i, j: (i, 0)) for k in k_sizes]
    row = 0
    for k in k_sizes:
        assert row % k == 0
        in_specs.append(pl.BlockSpec((None, k, TN), functools.partial(lambda i, j, rb: (layer, rb, j), rb=row // k)))
        row += k
    return pl.pallas_call(
        functools.partial(_mm_kernel, n_a=len(a_list)),
        out_shape=jax.ShapeDtypeStruct((T, N), out_dtype),
        grid=(T // tm, N // TN),
        in_specs=in_specs,
        out_specs=pl.BlockSpec((tm, TN), lambda i, j: (i, j)),
        compiler_params=_cparams(("arbitrary", "arbitrary"), 56),
        name="matmul",
    )(*a_list, *([w] * len(a_list)))


def _normalize_rows(x):
    mu = jnp.mean(x, axis=-1, keepdims=True)
    xc = x - mu
    var = jnp.mean(xc * xc, axis=-1, keepdims=True)
    return xc * lax.rsqrt(var + LN_EPS)


RET_HEADS = 2
RET_UNROLL = 4


def _ret_consts(lg_f, lg_b):
    C = CHUNK
    ii = lax.broadcasted_iota(jnp.int32, (C, C), 0).astype(F32)
    jj = lax.broadcasted_iota(jnp.int32, (C, C), 1).astype(F32)
    diff = ii - jj
    mask = (jnp.where(diff >= 0, jnp.exp(lg_f * jnp.maximum(diff, 0.0)), 0.0)
            + jnp.where(diff <= 0, jnp.exp(lg_b * jnp.maximum(-diff, 0.0)), 0.0))
    pos = lax.broadcasted_iota(jnp.int32, (C, 1), 0).astype(F32)
    return dict(
        mask=mask,
        qdec_f=jnp.exp(lg_f * (pos + 1.0)), kdec_f=jnp.exp(lg_f * (C - 1.0 - pos)),
        qdec_b=jnp.exp(lg_b * (C - pos)), kdec_b=jnp.exp(lg_b * pos),
        cdec_f=jnp.exp(jnp.full((1, HEAD_DIM), lg_f * C, F32)),
        cdec_b=jnp.exp(jnp.full((1, HEAD_DIM), lg_b * C, F32)))


def _ret_kernel(ld_ref, q_ref, k_ref, v_ref, g_ref, s0_ref, gnw_ref, o_ref, sout_ref, oacc_ref, *,
                layer, n_chunks, has_ctx):
    hb = pl.program_id(1)
    C = CHUNK
    consts = []
    for j in range(RET_HEADS):
        h = hb * RET_HEADS + j
        consts.append(_ret_consts(ld_ref[layer * 2 * H_RET + h], ld_ref[layer * 2 * H_RET + H_RET + h]))
    k_scale = HEAD_DIM ** -0.5
    nt = (((1,), (1,)), ((), ()))

    def load_chunk(c, j):
        rows = pl.ds(pl.multiple_of(c * C, C), C)
        cols = slice(j * HEAD_DIM, (j + 1) * HEAD_DIM)
        q = q_ref[rows, cols].astype(BF16)
        kf = k_ref[rows, cols] * k_scale
        v = v_ref[rows, cols].astype(BF16)
        return rows, cols, q, kf, v

    def state_update(s, kf, kdec, cdec, v):
        kd = (kf * kdec).astype(BF16)
        return s * cdec + jnp.dot(kd.T, v, preferred_element_type=F32)

    def fwd(c, states):
        out = []
        for j, s in enumerate(states):
            cj = consts[j]
            rows, cols, q, kf, v = load_chunk(c, j)
            qk = lax.dot_general(q, kf.astype(BF16), nt, preferred_element_type=F32)
            a = (qk * cj['mask']).astype(BF16)
            o = (jnp.dot(a, v, preferred_element_type=F32)
                 + jnp.dot(q, s.astype(BF16), preferred_element_type=F32) * cj['qdec_f'])
            oacc_ref[rows, cols] = o
            out.append(state_update(s, kf, cj['kdec_f'], cj['cdec_f'], v))
        return tuple(out)

    def bwd(t, states):
        c = n_chunks - 1 - t
        out = []
        for j, s in enumerate(states):
            cj = consts[j]
            rows, cols, q, kf, v = load_chunk(c, j)
            o = oacc_ref[rows, cols] + jnp.dot(q, s.astype(BF16), preferred_element_type=F32) * cj['qdec_b']
            y = _normalize_rows(o) * gnw_ref[:, cols] * _silu(g_ref[rows, cols])
            o_ref[rows, cols] = y.astype(o_ref.dtype)
            out.append(state_update(s, kf, cj['kdec_b'], cj['cdec_b'], v))
        return tuple(out)

    if has_ctx:
        s0_f = tuple(s0_ref[0, j] for j in range(RET_HEADS))
        s0_b = tuple(s0_ref[1, j] for j in range(RET_HEADS))
    else:
        s0_f = tuple(jnp.zeros((HEAD_DIM, HEAD_DIM), F32) for _ in range(RET_HEADS))
        s0_b = s0_f
    unroll = min(RET_UNROLL, n_chunks)
    s_f = lax.fori_loop(0, n_chunks, fwd, s0_f, unroll=unroll)
    s_b = lax.fori_loop(0, n_chunks, bwd, s0_b, unroll=unroll)
    for j in range(RET_HEADS):
        sout_ref[0, j] = s_f[j]
        sout_ref[1, j] = s_b[j]


def _retention(z3, log_decay_flat, state_ret, gn_w, layer, has_ctx):
    B, L, _ = z3.shape
    hp = RET_HEADS
    width = hp * HEAD_DIM
    col = lambda off: pl.BlockSpec((None, L, width), lambda b, h, ld: (b, 0, off // hp + h))
    state_block = (None, None, 2, hp, HEAD_DIM, HEAD_DIM)
    if has_ctx:
        s0 = state_ret
        s0_spec = pl.BlockSpec(state_block, lambda b, h, ld: (b, layer, 0, h, 0, 0))
    else:
        s0 = jnp.zeros((1, 1, 2, hp, HEAD_DIM, HEAD_DIM), F32)
        s0_spec = pl.BlockSpec(state_block, lambda b, h, ld: (0, 0, 0, 0, 0, 0))
    grid_spec = pltpu.PrefetchScalarGridSpec(
        num_scalar_prefetch=1,
        grid=(B, H_RET // hp),
        in_specs=[col(COL_RQ), col(COL_RK), col(COL_RV), col(COL_RG), s0_spec,
                  pl.BlockSpec((None, 1, width), lambda b, h, ld: (layer, 0, h))],
        out_specs=[pl.BlockSpec((None, L, width), lambda b, h, ld: (b, 0, h)),
                   pl.BlockSpec((None, 2, hp, HEAD_DIM, HEAD_DIM), lambda b, h, ld: (b, 0, h, 0, 0))],
        scratch_shapes=[pltpu.VMEM((L, width), F32)],
    )
    return pl.pallas_call(
        functools.partial(_ret_kernel, layer=layer, n_chunks=L // CHUNK, has_ctx=has_ctx),
        out_shape=[jax.ShapeDtypeStruct((B, L, D_RET), BF16),
                   jax.ShapeDtypeStruct((B, 2, H_RET, HEAD_DIM, HEAD_DIM), F32)],
        grid_spec=grid_spec,
        compiler_params=_cparams(("arbitrary", "arbitrary"), 52),
        name="retention",
    )(log_decay_flat, z3, z3, z3, z3, s0, gn_w.reshape(DEPTH, 1, D_RET))


def _sg_kernel(u_ref, v_ref, lnw_ref, ws_ref, bs_ref, o_ref, *, n_sub):
    for s in range(n_sub):
        rows = slice(s * CHUNK, (s + 1) * CHUNK)
        u = jax.nn.gelu(u_ref[rows, :])
        vn = (_normalize_rows(jax.nn.gelu(v_ref[rows, :])) * lnw_ref[...]).astype(BF16)
        for g in range(SM_GROUPS):
            cols = slice(g * LANES, (g + 1) * LANES)
            sg = jnp.dot(ws_ref[g], vn[:, cols], preferred_element_type=F32) + bs_ref[:, g:g + 1]
            o_ref[rows, cols] = (u[:, cols] * sg).astype(o_ref.dtype)


def _spatial_gating(z3, sm_ln_w, sm_w16, sm_bT, layer):
    B, L, _ = z3.shape
    ts = 2 * CHUNK
    return pl.pallas_call(
        functools.partial(_sg_kernel, n_sub=ts // CHUNK),
        out_shape=jax.ShapeDtypeStruct((B, L, D_SM), BF16),
        grid=(B, L // ts),
        in_specs=[
            pl.BlockSpec((None, ts, D_SM), lambda b, i: (b, i, COL_SU)),
            pl.BlockSpec((None, ts, D_SM), lambda b, i: (b, i, COL_SV)),
            pl.BlockSpec((None, 1, D_SM), lambda b, i: (layer, 0, 0)),
            pl.BlockSpec((None, SM_GROUPS, CHUNK, CHUNK), lambda b, i: (layer, 0, 0, 0)),
            pl.BlockSpec((None, CHUNK, SM_GROUPS), lambda b, i: (layer, 0, 0)),
        ],
        out_specs=pl.BlockSpec((None, ts, D_SM), lambda b, i: (b, i, 0)),
        compiler_params=_cparams(("arbitrary", "arbitrary"), 40),
        name="spatial_gating",
    )(z3, z3, sm_ln_w.reshape(DEPTH, 1, D_SM), sm_w16, sm_bT)


def _rms_rows(x, w):
    return x * lax.rsqrt(jnp.mean(x * x, axis=-1, keepdims=True) + RMS_EPS) * w


def _qk_kernel(*refs, rope):
    if rope:
        q_ref, k_ref, qw_ref, kw_ref, cos_ref, sin_ref, qo_ref, ko_ref = refs
        cos = cos_ref[...]
        sin = sin_ref[...]
        lane = lax.broadcasted_iota(jnp.int32, cos.shape, 1)
        first_half = (lane % (2 * ROPE_PAIRS)) < ROPE_PAIRS
    else:
        q_ref, k_ref, qw_ref, kw_ref, qo_ref, ko_ref, kf_ref = refs

    def rot(x):
        if not rope:
            return x
        partner = jnp.where(first_half, pltpu.roll(x, LANES - ROPE_PAIRS, axis=1), pltpu.roll(x, ROPE_PAIRS, axis=1))
        return x * cos + partner * sin

    for hd in range(H_ATT):
        cols = slice(hd * HEAD_DIM, (hd + 1) * HEAD_DIM)
        qo_ref[:, cols] = (rot(_rms_rows(q_ref[:, cols], qw_ref[...])) * Q_PRESCALE).astype(qo_ref.dtype)
    for hd in range(KV_HEADS):
        cols = slice(hd * HEAD_DIM, (hd + 1) * HEAD_DIM)
        kn = _rms_rows(k_ref[:, cols], kw_ref[...])
        if not rope:
            kf_ref[:, cols] = kn
        ko_ref[:, cols] = rot(kn).astype(ko_ref.dtype)


def _qk_prep(z3, q_norm_w, k_norm_w, layer, rope_tabs):
    B, L, _ = z3.shape
    tl = 256
    rope = rope_tabs is not None
    in_specs = [
        pl.BlockSpec((None, tl, D_ATT), lambda b, i: (b, i, COL_AQ)),
        pl.BlockSpec((None, tl, D_KV), lambda b, i: (b, i, COL_AK)),
        pl.BlockSpec((None, 1, HEAD_DIM), lambda b, i: (layer, 0, 0)),
        pl.BlockSpec((None, 1, HEAD_DIM), lambda b, i: (layer, 0, 0)),
    ]
    args = [z3, z3, q_norm_w.reshape(DEPTH, 1, HEAD_DIM), k_norm_w.reshape(DEPTH, 1, HEAD_DIM)]
    out_shape = [jax.ShapeDtypeStruct((B, L, D_ATT), BF16), jax.ShapeDtypeStruct((B, L, D_KV), BF16)]
    out_specs = [pl.BlockSpec((None, tl, D_ATT), lambda b, i: (b, i, 0)),
                 pl.BlockSpec((None, tl, D_KV), lambda b, i: (b, i, 0))]
    if rope:
        in_specs += [pl.BlockSpec((tl, HEAD_DIM), lambda b, i: (i, 0))] * 2
        args += list(rope_tabs)
    else:
        out_shape.append(jax.ShapeDtypeStruct((B, L, D_KV), F32))
        out_specs.append(pl.BlockSpec((None, tl, D_KV), lambda b, i: (b, i, 0)))
    return pl.pallas_call(
        functools.partial(_qk_kernel, rope=rope),
        out_shape=out_shape,
        grid=(B, L // tl),
        in_specs=in_specs,
        out_specs=out_specs,
        compiler_params=_cparams(("arbitrary", "arbitrary"), 40),
        name="qk_prep",
    )(*args)


def _rope_tables(n_tokens):
    rows = n_tokens // GRID_W
    row = jnp.broadcast_to(jnp.arange(rows, dtype=F32)[:, None], (rows, GRID_W)).reshape(-1)
    col = jnp.broadcast_to(jnp.arange(GRID_W, dtype=F32)[None, :], (rows, GRID_W)).reshape(-1)
    inv = ROPE_THETA ** (-2.0 * jnp.arange(ROPE_PAIRS, dtype=F32) / (2 * ROPE_PAIRS))
    ang_r = row[:, None] * inv
    ang_c = col[:, None] * inv
    cos = jnp.concatenate([jnp.cos(ang_r), jnp.cos(ang_r), jnp.cos(ang_c), jnp.cos(ang_c)], axis=1)
    sin = jnp.concatenate([-jnp.sin(ang_r), jnp.sin(ang_r), -jnp.sin(ang_c), jnp.sin(ang_c)], axis=1)
    return cos, sin


def _att_kernel(q_ref, k_ref, v1_ref, o_ref):
    k = k_ref[...]
    v1 = v1_ref[...]
    nt = (((1,), (1,)), ((), ()))
    for g in range(Q_PER_KV):
        cols = slice(g * HEAD_DIM, (g + 1) * HEAD_DIM)
        for r0 in range(0, q_ref.shape[0], ATT_UNIT_ROWS):
            rows = slice(r0, r0 + ATT_UNIT_ROWS)
            s = lax.dot_general(q_ref[rows, cols], k, nt, preferred_element_type=F32)
            p = jnp.exp2(s - jnp.max(s, axis=-1, keepdims=True)).astype(BF16)
            ol = jnp.dot(p, v1, preferred_element_type=F32)
            o_ref[rows, cols] = (ol[:, :HEAD_DIM] / ol[:, HEAD_DIM:]).astype(o_ref.dtype)


ATT_UNIT_ROWS = 256
ATT_UNITS_PER_HEAD = 4


def _attention(q, keys, vals):
    B, L, _ = q.shape
    Lk = keys.shape[1]
    tq = min(ATT_UNITS_PER_HEAD * ATT_UNIT_ROWS, L)
    blk = Q_PER_KV * HEAD_DIM
    v4 = vals.reshape(B, Lk, KV_HEADS, HEAD_DIM)
    vals1 = jnp.concatenate([v4, jnp.ones_like(v4)], axis=-1).reshape(B, Lk, 2 * D_KV)
    return pl.pallas_call(
        _att_kernel,
        out_shape=jax.ShapeDtypeStruct((B, L, D_ATT), BF16),
        grid=(B, KV_HEADS, L // tq),
        in_specs=[
            pl.BlockSpec((None, tq, blk), lambda b, h, i: (b, i, h)),
            pl.BlockSpec((None, Lk, HEAD_DIM), lambda b, h, i: (b, 0, h)),
            pl.BlockSpec((None, Lk, 2 * HEAD_DIM), lambda b, h, i: (b, 0, h)),
        ],
        out_specs=pl.BlockSpec((None, tq, blk), lambda b, h, i: (b, i, h)),
        compiler_params=_cparams(("arbitrary", "arbitrary", "arbitrary"), 48),
        name="attention",
    )(q, keys, vals1)


def _ln_kernel(*refs, n_y, gate_idx, mod_idx, with_router):
    x_ref = refs[0]
    y_refs = refs[1:1 + n_y]
    mg_ref, lnw_ref, lnb_ref = refs[1 + n_y:4 + n_y]
    rest = list(refs[4 + n_y:])
    y = y_refs[0][...]
    for r in y_refs[1:]:
        y = y + r[...]
    t = DEEPNORM_ALPHA * x_ref[...] + mg_ref[gate_idx:gate_idx + 1, :] * y
    xn = _normalize_rows(t) * lnw_ref[...] + lnb_ref[...]
    if mod_idx is None:
        (xo_ref,) = rest
        xo_ref[...] = xn
        return
    mh_ref = rest.pop(0)
    h = xn * (1.0 + mh_ref[mod_idx + 1:mod_idx + 2, :]) + mh_ref[mod_idx:mod_idx + 1, :]
    if with_router:
        wh_ref, wl_ref, br_ref, xo_ref, ho_ref, lg_ref = rest
        h_hi = h.astype(BF16)
        h_lo = (h - h_hi.astype(F32)).astype(BF16)
        w_hi = wh_ref[...]
        lg_ref[...] = (jnp.dot(h_hi, w_hi, preferred_element_type=F32)
                       + (jnp.dot(h_lo, w_hi, preferred_element_type=F32)
                          + jnp.dot(h_hi, wl_ref[...], preferred_element_type=F32))
                       + br_ref[...])
    else:
        xo_ref, ho_ref = rest
    xo_ref[...] = xn
    ho_ref[...] = h.astype(ho_ref.dtype)


def _ln_block(x, ys, mods_gate, gate_idx, ln_w, ln_b, layer, mods_h=None, mod_idx=None, router=None):
    B, L, D = x.shape
    tl = LN_ROWS
    row_spec = pl.BlockSpec((None, tl, D), lambda b, i: (b, i, 0))
    in_specs = [row_spec]
    args = [x]
    for arr, first in ys:
        in_specs.append(pl.BlockSpec((tl, D), functools.partial(lambda b, i, first: (first + b * (L // tl) + i, 0),
                                                                first=first)))
        args.append(arr)
    vec_spec = pl.BlockSpec((None, 1, D), lambda b, i: (layer, 0, 0))
    in_specs += [pl.BlockSpec((None, N_MOD, D), _mods_index(mods_gate.shape[0])), vec_spec, vec_spec]
    args += [mods_gate, ln_w.reshape(DEPTH, 1, D), ln_b.reshape(DEPTH, 1, D)]
    out_shape = [jax.ShapeDtypeStruct((B, L, D), F32)]
    out_specs = [row_spec]
    if mod_idx is not None:
        in_specs.append(pl.BlockSpec((None, N_MOD, D), _mods_index(mods_h.shape[0])))
        args.append(mods_h)
        if router is not None:
            w_hi, w_lo, b_r = router
            w_spec = pl.BlockSpec((None, D, ROUTER_PAD), lambda b, i: (layer, 0, 0))
            in_specs += [w_spec, w_spec, pl.BlockSpec((None, 1, ROUTER_PAD), lambda b, i: (layer, 0, 0))]
            args += [w_hi, w_lo, b_r]
            out_shape += [jax.ShapeDtypeStruct((B, L, D), F32), jax.ShapeDtypeStruct((B, L, ROUTER_PAD), F32)]
            out_specs += [row_spec, pl.BlockSpec((None, tl, ROUTER_PAD), lambda b, i: (b, i, 0))]
        else:
            out_shape.append(jax.ShapeDtypeStruct((B, L, D), BF16))
            out_specs.append(row_spec)
    return pl.pallas_call(
        functools.partial(_ln_kernel, n_y=len(ys), gate_idx=gate_idx, mod_idx=mod_idx,
                          with_router=router is not None),
        out_shape=out_shape,
        grid=(B, L // tl),
        in_specs=in_specs,
        out_specs=out_specs,
        compiler_params=_cparams(("arbitrary", "arbitrary"), 48),
        name="layernorm",
    )(*args)


MOE_BUFS = 3


def _up_kernel(be_ref, nu_ref, tok_ref, x_hbm, wg_ref, wu_ref, o_ref, xbuf, sem):
    b = pl.program_id(0)
    n_used = nu_ref[0]

    def row_copy(block, r):
        to = block % MOE_BUFS
        t = tok_ref[block * MOE_ROWS + r]
        return pltpu.make_async_copy(x_hbm.at[pl.ds(t, 1), :], xbuf.at[to, pl.ds(r, 1), :], sem.at[to])

    def issue_loop(block):
        def body(r, carry):
            row_copy(block, r).start()
            return carry

        lax.fori_loop(0, MOE_ROWS, body, 0)

    def wait_block(block):
        s = block % MOE_BUFS
        pltpu.make_async_copy(x_hbm.at[pl.ds(0, MOE_ROWS), :], xbuf.at[s], sem.at[s]).wait()

    @pl.when(b == 0)
    def _():
        issue_loop(0)

        @pl.when(1 < n_used)
        def _():
            issue_loop(1)

        wait_block(0)

    def compute(prefetch):
        x = xbuf[b % MOE_BUFS].astype(BF16)
        if prefetch:
            for r in range(MOE_ROWS):
                row_copy(b + 2, r).start()
        hg = jnp.dot(x, wg_ref[...], preferred_element_type=F32)
        hu = jnp.dot(x, wu_ref[...], preferred_element_type=F32)
        o_ref[...] = (_silu(hg) * hu).astype(o_ref.dtype)

    @pl.when(b + 2 < n_used)
    def _():
        compute(True)

    @pl.when((b < n_used) & (b + 2 >= n_used))
    def _():
        compute(False)

    @pl.when(b >= n_used)
    def _():
        o_ref[...] = jnp.zeros_like(o_ref)

    @pl.when(b + 1 < n_used)
    def _():
        wait_block(b + 1)


def _down_kernel(be_ref, nu_ref, dst_ref, a_ref, wd_ref, sw_ref, out_hbm, y0, y1, y2, sem, *, n_rows):
    b = pl.program_id(0)
    n_used = nu_ref[0]
    bufs = (y0, y1, y2)

    def row_copy(k, block, r):
        d = dst_ref[block * MOE_ROWS + r]
        return pltpu.make_async_copy(bufs[k].at[pl.ds(r, 1), :], out_hbm.at[pl.ds(d, 1), :], sem.at[k])

    def product():
        return jnp.dot(a_ref[...], wd_ref[...], preferred_element_type=F32) * sw_ref[...]

    @pl.when(b == 0)
    def _():
        for s in range(2):
            bufs[s][...] = jnp.zeros_like(bufs[s])
            fill = pltpu.make_async_copy(bufs[s], out_hbm.at[pl.ds(n_rows + s * MOE_ROWS, MOE_ROWS), :], sem.at[s])
            fill.start()
            fill.wait()
        y0[...] = product()

    for k in range(MOE_BUFS):
        prev = (k - 1) % MOE_BUFS

        @pl.when((b % MOE_BUFS == k) & (b >= 1) & (b < n_used))
        def _():
            for r in range(MOE_ROWS):
                row_copy(prev, b - 1, r).start()
            bufs[k][...] = product()

        @pl.when((b % MOE_BUFS == k) & (b == n_used))
        def _():
            def body(r, carry):
                row_copy(prev, b - 1, r).start()
                return carry

            lax.fori_loop(0, MOE_ROWS, body, 0)

    for k in range(MOE_BUFS):
        @pl.when(((b + 1) % MOE_BUFS == k) & (b >= 2) & (b - 2 < n_used))
        def _():
            pltpu.make_async_copy(bufs[k], out_hbm.at[pl.ds(0, MOE_ROWS), :], sem.at[k]).wait()


def _experts(h2, slot_tok, slot_dst, block_e, n_used, slot_w, w_up16, w_down16, layer):
    T = h2.shape[0]
    n_slots = slot_tok.shape[0]
    nb = n_slots // MOE_ROWS
    up_spec = pltpu.PrefetchScalarGridSpec(
        num_scalar_prefetch=3,
        grid=(nb,),
        in_specs=[
            pl.BlockSpec(memory_space=pl.ANY),
            pl.BlockSpec((None, None, D_MODEL, D_EXPERT), lambda b, be, nu, tok: (layer, be[b], 0, 0)),
            pl.BlockSpec((None, None, D_MODEL, D_EXPERT), lambda b, be, nu, tok: (layer, be[b], 0, 1)),
        ],
        out_specs=pl.BlockSpec((MOE_ROWS, D_EXPERT), lambda b, be, nu, tok: (b, 0)),
        scratch_shapes=[pltpu.VMEM((MOE_BUFS, MOE_ROWS, D_MODEL), F32), pltpu.SemaphoreType.DMA((MOE_BUFS,))],
    )
    act = pl.pallas_call(
        _up_kernel,
        out_shape=jax.ShapeDtypeStruct((n_slots, D_EXPERT), BF16),
        grid_spec=up_spec,
        compiler_params=_cparams(("arbitrary",), 56),
        name="expert_up",
    )(block_e, n_used, slot_tok, h2, w_up16, w_up16)
    down_spec = pltpu.PrefetchScalarGridSpec(
        num_scalar_prefetch=3,
        grid=(nb,),
        in_specs=[
            pl.BlockSpec((MOE_ROWS, D_EXPERT), lambda b, be, nu, dst: (b, 0)),
            pl.BlockSpec((None, None, D_EXPERT, D_MODEL), lambda b, be, nu, dst: (layer, be[b], 0, 0)),
            pl.BlockSpec((MOE_ROWS, 1), lambda b, be, nu, dst: (b, 0)),
        ],
        out_specs=pl.BlockSpec(memory_space=pl.ANY),
        scratch_shapes=[pltpu.VMEM((MOE_ROWS, D_MODEL), F32)] * MOE_BUFS + [pltpu.SemaphoreType.DMA((MOE_BUFS,))],
    )
    return pl.pallas_call(
        functools.partial(_down_kernel, n_rows=2 * T),
        out_shape=jax.ShapeDtypeStruct((2 * T + 2 * MOE_ROWS, D_MODEL), F32),
        grid_spec=down_spec,
        compiler_params=_cparams(("arbitrary",), 48),
        name="expert_down",
    )(block_e, n_used, slot_dst, act, w_down16, slot_w.reshape(n_slots, 1))


def _route(logits):
    T = logits.shape[0]
    lc = logits[:, :N_GROUPS]
    lf = logits[:, N_GROUPS:N_GROUPS + N_EXPERTS].reshape(T, N_GROUPS, EXPERTS_PER_GROUP)
    grp = jnp.argmax(lc, axis=-1).astype(jnp.int32)
    p_grp = jnp.take_along_axis(jax.nn.softmax(lc, axis=-1), grp[:, None], axis=-1)
    lfg = jnp.take_along_axis(lf, grp[:, None, None], axis=1)[:, 0]
    top_v, top_i = lax.top_k(lfg, 2)
    gate = jax.nn.softmax(top_v, axis=-1) * p_grp
    eid = (grp[:, None] * EXPERTS_PER_GROUP + top_i.astype(jnp.int32)).T.reshape(-1)
    wgt = gate.T.reshape(-1)
    n_assign = 2 * T
    experts = jnp.arange(N_EXPERTS, dtype=jnp.int32)
    counts = jnp.sum(eid[:, None] == experts[None, :], axis=0, dtype=jnp.int32)
    start = jnp.cumsum(counts) - counts
    pcounts = (counts + MOE_ROWS - 1) // MOE_ROWS * MOE_ROWS
    pend = jnp.cumsum(pcounts)
    pstart = pend - pcounts
    order = jnp.argsort(eid).astype(jnp.int32)
    n_blocks = n_assign // MOE_ROWS + N_EXPERTS + 2
    block_first = jnp.arange(n_blocks, dtype=jnp.int32) * MOE_ROWS
    block_e = jnp.minimum(jnp.sum(pend[None, :] <= block_first[:, None], axis=1, dtype=jnp.int32), N_EXPERTS - 1)
    r = (block_first - pstart[block_e])[:, None] + jnp.arange(MOE_ROWS, dtype=jnp.int32)[None, :]
    valid = r < counts[block_e][:, None]
    src = order[jnp.clip(start[block_e][:, None] + r, 0, n_assign - 1)]
    in_block = jnp.arange(MOE_ROWS, dtype=jnp.int32)[None, :]
    slot = block_first[:, None] + in_block
    spare = n_assign + (jnp.arange(n_blocks, dtype=jnp.int32) % 2)[:, None] * MOE_ROWS + in_block
    slot_tok = jnp.where(valid, src % T, slot % T).reshape(-1)
    slot_dst = jnp.where(valid, src, spare).reshape(-1)
    slot_w = jnp.where(valid, wgt[src], 0.0).reshape(-1)
    n_used = (pend[-1:] // MOE_ROWS).astype(jnp.int32)
    return slot_tok, slot_dst, slot_w, block_e, n_used


def _layer(x, h, mods, mods_next, W, layer, ctx, rope_tabs):
    B, L, D = x.shape
    T = B * L
    z = _matmul([h.reshape(T, D)], W['w_in'], layer, (D,), F32)
    z3 = z.reshape(B, L, N_IN)

    ret, s_new = _retention(z3, W['log_decay'], ctx[0] if ctx else None, W['ret_gn_w'], layer, ctx is not None)
    sm = _spatial_gating(z3, W['sm_ln_w'], W['sm_w'], W['sm_bT'], layer)
    v_new = z3[:, :, OFF_AV:OFF_AV + D_KV]
    if ctx is None:
        q16, k16, k_new = _qk_prep(z3, W['q_norm_w'], W['k_norm_w'], layer, None)
        keys, vals = k16, v_new.astype(BF16)
        new_ctx = (s_new, k_new.reshape(B, L, KV_HEADS, HEAD_DIM), v_new.reshape(B, L, KV_HEADS, HEAD_DIM))
    else:
        q16, k16 = _qk_prep(z3, W['q_norm_w'], W['k_norm_w'], layer, rope_tabs)
        keys = jnp.concatenate([k16, ctx[1]], axis=1)
        vals = jnp.concatenate([v_new.astype(BF16), ctx[2]], axis=1)
        new_ctx = None
    att = _attention(q16, keys, vals)

    mix = _matmul([ret.reshape(T, D_RET), sm.reshape(T, D_SM), att.reshape(T, D_ATT)], W['w_out'], layer,
                  (D_RET, D_SM, D_ATT), F32)
    x1, h2, logits = _ln_block(x, [(mix, 0)], mods, 2, W['ln1_w'], W['ln1_b'], layer,
                               mods_h=mods, mod_idx=3, router=(W['router_w_hi'], W['router_w_lo'], W['router_b']))

    slot_tok, slot_dst, slot_w, block_e, n_used = _route(logits.reshape(T, ROUTER_PAD))
    ffn = _experts(h2.reshape(T, D), slot_tok, slot_dst, block_e, n_used, slot_w, W['w_up'], W['w_down'], layer)
    ys = [(ffn, 0), (ffn, T // LN_ROWS)]
    if mods_next is None:
        (x2,) = _ln_block(x1, ys, mods, 5, W['ln2_w'], W['ln2_b'], layer)
        h_next = None
    else:
        x2, h_next = _ln_block(x1, ys, mods, 5, W['ln2_w'], W['ln2_b'], layer, mods_h=mods_next, mod_idx=0)
    return x2, h_next, new_ctx


def kernel(x_prompt, x_sample, state_ret, cache_k, cache_v, c, c_ctx, w_mod, b_mod, w_in, ret_log_decay, ret_gn_w, sm_ln_w, sm_w, sm_b, q_norm_w, k_norm_w, w_out, ln1_w, ln1_b, router_w_coarse, router_b_coarse, router_w_fine, router_b_fine, w_up, w_down, ln2_w, ln2_b):
    n_dec = c.shape[0]
    cond = jnp.zeros((16, D_MODEL), F32).at[:n_dec].set(c).at[n_dec].set(c_ctx)
    mods_all = _adaln_mods(cond, w_mod, b_mod)
    mods_lat = [mods_all[l, :n_dec].reshape(n_dec, N_MOD, D_MODEL) for l in range(DEPTH)]
    mods_ctx = [mods_all[l, n_dec:n_dec + 1].reshape(1, N_MOD, D_MODEL) for l in range(DEPTH)]

    n_r = N_GROUPS + N_EXPERTS
    router_w = jnp.zeros((DEPTH, D_MODEL, ROUTER_PAD), F32).at[:, :, :n_r].set(
        jnp.concatenate([router_w_coarse, router_w_fine], axis=-1))
    router_b = jnp.zeros((DEPTH, 1, ROUTER_PAD), F32).at[:, 0, :n_r].set(
        jnp.concatenate([router_b_coarse, router_b_fine], axis=-1))
    router_w_hi = router_w.astype(BF16)
    W = {
        'w_in': w_in.astype(BF16), 'w_out': w_out.astype(BF16),
        'w_up': w_up.astype(BF16), 'w_down': w_down.astype(BF16),
        'log_decay': ret_log_decay.reshape(-1), 'ret_gn_w': ret_gn_w,
        'sm_ln_w': sm_ln_w, 'sm_w': sm_w.astype(BF16), 'sm_bT': jnp.swapaxes(sm_b, 1, 2),
        'q_norm_w': q_norm_w, 'k_norm_w': k_norm_w,
        'ln1_w': ln1_w, 'ln1_b': ln1_b, 'ln2_w': ln2_w, 'ln2_b': ln2_b,
        'router_w_hi': router_w_hi, 'router_w_lo': (router_w - router_w_hi.astype(F32)).astype(BF16),
        'router_b': router_b,
    }
    rope_tabs = _rope_tables(x_sample.shape[1])
    past = cache_k.shape[2]
    ck16 = cache_k.astype(BF16).reshape(n_dec, DEPTH, past, D_KV)
    cv16 = cache_v.astype(BF16).reshape(n_dec, DEPTH, past, D_KV)

    y_p, y_s = x_prompt, x_sample
    h_p = _modulate(x_prompt, mods_ctx[0])
    h_s = _modulate(x_sample, mods_lat[0])
    states, ctx_k, ctx_v = [], [], []
    for l in range(DEPTH):
        nxt = l + 1 < DEPTH
        y_p, h_p, (s_l, k_l, v_l) = _layer(y_p, h_p, mods_ctx[l], mods_ctx[l + 1] if nxt else None, W, l, None, None)
        states.append(s_l)
        ctx_k.append(k_l)
        ctx_v.append(v_l)
        y_s, h_s, _ = _layer(y_s, h_s, mods_lat[l], mods_lat[l + 1] if nxt else None, W, l,
                             (state_ret, ck16[:, l], cv16[:, l]), rope_tabs)
    return (y_p, y_s, jnp.stack(states, axis=1), jnp.stack(ctx_k, axis=1), jnp.stack(ctx_v, axis=1))
```

```python
import functools

import numpy as np
import jax
import jax.numpy as jnp
from jax import lax
from jax.experimental import pallas as pl
from jax.experimental.pallas import tpu as pltpu

F32 = jnp.float32
BF16 = jnp.bfloat16

D_MODEL = 4096
DEPTH = 2
GRID_W = 64
HEAD_DIM = 128
H_RET = 8
D_RET = H_RET * HEAD_DIM
SM_GROUPS = 8
D_SM = 1024
H_ATT = 16
KV_HEADS = 4
Q_PER_KV = H_ATT // KV_HEADS
D_ATT = H_ATT * HEAD_DIM
D_KV = KV_HEADS * HEAD_DIM
CHUNK = 128
ROPE_PAIRS = 32
ROPE_THETA = 10000.0
N_GROUPS = 4
EXPERTS_PER_GROUP = 8
N_EXPERTS = N_GROUPS * EXPERTS_PER_GROUP
D_EXPERT = D_MODEL // 4
N_MOD = 6
N_IN = D_RET * 4 + D_SM * 2 + D_ATT + D_KV * 2
DEEPNORM_ALPHA = (2 * DEPTH) ** 0.25
LN_EPS = 1e-5
RMS_EPS = 1e-6
Q_PRESCALE = float(HEAD_DIM ** -0.5 * np.log2(np.e))

COL_RQ, COL_RK, COL_RV, COL_RG = 0, H_RET, 2 * H_RET, 3 * H_RET
COL_SU, COL_SV = 4, 5
COL_AQ = 3
COL_AK, COL_AV = 16, 17
OFF_AV = COL_AV * D_KV

LANES = 128
ROUTER_PAD = LANES
MIB = 1024 * 1024
MOE_ROWS = 256
TM = 1024
TN = 1024
LN_ROWS = 128
QK_ROWS = 512
MOD_ROWS = 512


def _cparams(sem, vmem_mib):
    return pltpu.CompilerParams(dimension_semantics=sem, vmem_limit_bytes=vmem_mib * MIB)


def _sigmoid(x):
    return 1.0 / (1.0 + jnp.exp(-x))


def _silu(x):
    return x * _sigmoid(x)


def _mods_kernel(c_ref, w_ref, b_ref, o_ref):
    a = _silu(c_ref[...]).astype(BF16)
    o_ref[...] = jnp.dot(a, w_ref[...].astype(BF16), preferred_element_type=F32) + b_ref[...]


def _adaln_mods(cond, w_mod, b_mod):
    n = cond.shape[0]
    tn = 512
    n_out = N_MOD * D_MODEL
    return pl.pallas_call(
        _mods_kernel,
        out_shape=jax.ShapeDtypeStruct((DEPTH, n, n_out), F32),
        grid=(DEPTH, n_out // tn),
        in_specs=[
            pl.BlockSpec((n, D_MODEL), lambda l, j: (0, 0)),
            pl.BlockSpec((None, D_MODEL, tn), lambda l, j: (l, 0, j)),
            pl.BlockSpec((None, 1, tn), lambda l, j: (l, 0, j)),
        ],
        out_specs=pl.BlockSpec((None, n, tn), lambda l, j: (l, 0, j)),
        compiler_params=_cparams(("arbitrary", "arbitrary"), 40),
        name="adaln_mods",
    )(cond, w_mod, b_mod.reshape(DEPTH, 1, n_out))


def _modulate_kernel(x_ref, m_ref, o_ref):
    o_ref[...] = (x_ref[...] * (1.0 + m_ref[1:2, :]) + m_ref[0:1, :]).astype(BF16)


def _mods_index(bm):
    if bm == 1:
        return lambda b, i: (0, 0, 0)
    return lambda b, i: (b, 0, 0)


def _modulate(x, mods):
    B, L, D = x.shape
    tl = min(MOD_ROWS, L)
    return pl.pallas_call(
        _modulate_kernel,
        out_shape=jax.ShapeDtypeStruct((B, L, D), BF16),
        grid=(B, L // tl),
        in_specs=[
            pl.BlockSpec((None, tl, D), lambda b, i: (b, i, 0)),
            pl.BlockSpec((None, N_MOD, D), _mods_index(mods.shape[0])),
        ],
        out_specs=pl.BlockSpec((None, tl, D), lambda b, i: (b, i, 0)),
        compiler_params=_cparams(("arbitrary", "arbitrary"), 40),
        name="modulate",
    )(x, mods)


def _mm_kernel(*refs, n_a):
    o_ref = refs[2 * n_a]
    acc = None
    for a_ref, w_ref in zip(refs[:n_a], refs[n_a:2 * n_a]):
        d = jnp.dot(a_ref[...], w_ref[...], preferred_element_type=F32)
        acc = d if acc is None else acc + d
    o_ref[...] = acc.astype(o_ref.dtype)


def _matmul(a_list, w, layer, k_sizes, out_dtype):
    T = a_list[0].shape[0]
    N = w.shape[2]
    tm = min(TM, T)
    in_specs = [pl.BlockSpec((tm, k), lambda i, j: (i, 0)) for k in k_sizes]
    row = 0
    for k in k_sizes:
        assert row % k == 0
        in_specs.append(pl.BlockSpec((None, k, TN), functools.partial(lambda i, j, rb: (layer, rb, j), rb=row // k)))
        row += k
    return pl.pallas_call(
        functools.partial(_mm_kernel, n_a=len(a_list)),
        out_shape=jax.ShapeDtypeStruct((T, N), out_dtype),
        grid=(T // tm, N // TN),
        in_specs=in_specs,
        out_specs=pl.BlockSpec((tm, TN), lambda i, j: (i, j)),
        compiler_params=_cparams(("arbitrary", "arbitrary"), 56),
        name="matmul",
    )(*a_list, *([w] * len(a_list)))


def _normalize_rows(x):
    mu = jnp.mean(x, axis=-1, keepdims=True)
    xc = x - mu
    var = jnp.mean(xc * xc, axis=-1, keepdims=True)
    return xc * lax.rsqrt(var + LN_EPS)


RET_HEADS = 2
RET_UNROLL = 4


def _ret_consts(lg_f, lg_b):
    C = CHUNK
    ii = lax.broadcasted_iota(jnp.int32, (C, C), 0).astype(F32)
    jj = lax.broadcasted_iota(jnp.int32, (C, C), 1).astype(F32)
    diff = ii - jj
    mask = (jnp.where(diff >= 0, jnp.exp(lg_f * jnp.maximum(diff, 0.0)), 0.0)
            + jnp.where(diff <= 0, jnp.exp(lg_b * jnp.maximum(-diff, 0.0)), 0.0))
    pos = lax.broadcasted_iota(jnp.int32, (C, 1), 0).astype(F32)
    return dict(
        mask=mask,
        qdec_f=jnp.exp(lg_f * (pos + 1.0)), kdec_f=jnp.exp(lg_f * (C - 1.0 - pos)),
        qdec_b=jnp.exp(lg_b * (C - pos)), kdec_b=jnp.exp(lg_b * pos),
        cdec_f=jnp.exp(jnp.full((1, HEAD_DIM), lg_f * C, F32)),
        cdec_b=jnp.exp(jnp.full((1, HEAD_DIM), lg_b * C, F32)))


def _ret_kernel(ld_ref, q_ref, k_ref, v_ref, g_ref, s0_ref, gnw_ref, o_ref, sout_ref, oacc_ref, *,
                layer, n_chunks, has_ctx):
    hb = pl.program_id(1)
    C = CHUNK
    consts = []
    for j in range(RET_HEADS):
        h = hb * RET_HEADS + j
        consts.append(_ret_consts(ld_ref[layer * 2 * H_RET + h], ld_ref[layer * 2 * H_RET + H_RET + h]))
    k_scale = HEAD_DIM ** -0.5
    nt = (((1,), (1,)), ((), ()))

    def load_chunk(c, j):
        rows = pl.ds(pl.multiple_of(c * C, C), C)
        cols = slice(j * HEAD_DIM, (j + 1) * HEAD_DIM)
        q = q_ref[rows, cols].astype(BF16)
        kf = k_ref[rows, cols] * k_scale
        v = v_ref[rows, cols].astype(BF16)
        return rows, cols, q, kf, v

    def state_update(s, kf, kdec, cdec, v):
        kd = (kf * kdec).astype(BF16)
        return s * cdec + jnp.dot(kd.T, v, preferred_element_type=F32)

    def fwd(c, states):
        out = []
        for j, s in enumerate(states):
            cj = consts[j]
            rows, cols, q, kf, v = load_chunk(c, j)
            qk = lax.dot_general(q, kf.astype(BF16), nt, preferred_element_type=F32)
            a = (qk * cj['mask']).astype(BF16)
            o = (jnp.dot(a, v, preferred_element_type=F32)
                 + jnp.dot(q, s.astype(BF16), preferred_element_type=F32) * cj['qdec_f'])
            oacc_ref[rows, cols] = o
            out.append(state_update(s, kf, cj['kdec_f'], cj['cdec_f'], v))
        return tuple(out)

    def bwd(t, states):
        c = n_chunks - 1 - t
        out = []
        for j, s in enumerate(states):
            cj = consts[j]
            rows, cols, q, kf, v = load_chunk(c, j)
            o = oacc_ref[rows, cols] + jnp.dot(q, s.astype(BF16), preferred_element_type=F32) * cj['qdec_b']
            y = _normalize_rows(o) * gnw_ref[:, cols] * _silu(g_ref[rows, cols])
            o_ref[rows, cols] = y.astype(o_ref.dtype)
            out.append(state_update(s, kf, cj['kdec_b'], cj['cdec_b'], v))
        return tuple(out)

    if has_ctx:
        s0_f = tuple(s0_ref[0, j] for j in range(RET_HEADS))
        s0_b = tuple(s0_ref[1, j] for j in range(RET_HEADS))
    else:
        s0_f = tuple(jnp.zeros((HEAD_DIM, HEAD_DIM), F32) for _ in range(RET_HEADS))
        s0_b = s0_f
    unroll = min(RET_UNROLL, n_chunks)
    s_f = lax.fori_loop(0, n_chunks, fwd, s0_f, unroll=unroll)
    s_b = lax.fori_loop(0, n_chunks, bwd, s0_b, unroll=unroll)
    for j in range(RET_HEADS):
        sout_ref[0, j] = s_f[j]
        sout_ref[1, j] = s_b[j]


def _retention(z3, log_decay_flat, state_ret, gn_w, layer, has_ctx):
    B, L, _ = z3.shape
    hp = RET_HEADS
    width = hp * HEAD_DIM
    col = lambda off: pl.BlockSpec((None, L, width), lambda b, h, ld: (b, 0, off // hp + h))
    state_block = (None, None, 2, hp, HEAD_DIM, HEAD_DIM)
    if has_ctx:
        s0 = state_ret
        s0_spec = pl.BlockSpec(state_block, lambda b, h, ld: (b, layer, 0, h, 0, 0))
    else:
        s0 = jnp.zeros((1, 1, 2, hp, HEAD_DIM, HEAD_DIM), F32)
        s0_spec = pl.BlockSpec(state_block, lambda b, h, ld: (0, 0, 0, 0, 0, 0))
    grid_spec = pltpu.PrefetchScalarGridSpec(
        num_scalar_prefetch=1,
        grid=(B, H_RET // hp),
        in_specs=[col(COL_RQ), col(COL_RK), col(COL_RV), col(COL_RG), s0_spec,
                  pl.BlockSpec((None, 1, width), lambda b, h, ld: (layer, 0, h))],
        out_specs=[pl.BlockSpec((None, L, width), lambda b, h, ld: (b, 0, h)),
                   pl.BlockSpec((None, 2, hp, HEAD_DIM, HEAD_DIM), lambda b, h, ld: (b, 0, h, 0, 0))],
        scratch_shapes=[pltpu.VMEM((L, width), F32)],
    )
    return pl.pallas_call(
        functools.partial(_ret_kernel, layer=layer, n_chunks=L // CHUNK, has_ctx=has_ctx),
        out_shape=[jax.ShapeDtypeStruct((B, L, D_RET), BF16),
                   jax.ShapeDtypeStruct((B, 2, H_RET, HEAD_DIM, HEAD_DIM), F32)],
        grid_spec=grid_spec,
        compiler_params=_cparams(("arbitrary", "arbitrary"), 52),
        name="retention",
    )(log_decay_flat, z3, z3, z3, z3, s0, gn_w.reshape(DEPTH, 1, D_RET))


def _sg_kernel(u_ref, v_ref, lnw_ref, ws_ref, bs_ref, o_ref, *, n_sub):
    for s in range(n_sub):
        rows = slice(s * CHUNK, (s + 1) * CHUNK)
        u = jax.nn.gelu(u_ref[rows, :])
        vn = (_normalize_rows(jax.nn.gelu(v_ref[rows, :])) * lnw_ref[...]).astype(BF16)
        for g in range(SM_GROUPS):
            cols = slice(g * LANES, (g + 1) * LANES)
            sg = jnp.dot(ws_ref[g], vn[:, cols], preferred_element_type=F32) + bs_ref[:, g:g + 1]
            o_ref[rows, cols] = (u[:, cols] * sg).astype(o_ref.dtype)


def _spatial_gating(z3, sm_ln_w, sm_w16, sm_bT, layer):
    B, L, _ = z3.shape
    ts = 2 * CHUNK
    return pl.pallas_call(
        functools.partial(_sg_kernel, n_sub=ts // CHUNK),
        out_shape=jax.ShapeDtypeStruct((B, L, D_SM), BF16),
        grid=(B, L // ts),
        in_specs=[
            pl.BlockSpec((None, ts, D_SM), lambda b, i: (b, i, COL_SU)),
            pl.BlockSpec((None, ts, D_SM), lambda b, i: (b, i, COL_SV)),
            pl.BlockSpec((None, 1, D_SM), lambda b, i: (layer, 0, 0)),
            pl.BlockSpec((None, SM_GROUPS, CHUNK, CHUNK), lambda b, i: (layer, 0, 0, 0)),
            pl.BlockSpec((None, CHUNK, SM_GROUPS), lambda b, i: (layer, 0, 0)),
        ],
        out_specs=pl.BlockSpec((None, ts, D_SM), lambda b, i: (b, i, 0)),
        compiler_params=_cparams(("arbitrary", "arbitrary"), 40),
        name="spatial_gating",
    )(z3, z3, sm_ln_w.reshape(DEPTH, 1, D_SM), sm_w16, sm_bT)


def _rms_rows(x, w):
    return x * lax.rsqrt(jnp.mean(x * x, axis=-1, keepdims=True) + RMS_EPS) * w


def _qk_kernel(*refs, rope):
    if rope:
        q_ref, k_ref, qw_ref, kw_ref, cos_ref, sin_ref, qo_ref, ko_ref = refs
        cos = cos_ref[...]
        sin = sin_ref[...]
        lane = lax.broadcasted_iota(jnp.int32, cos.shape, 1)
        first_half = (lane % (2 * ROPE_PAIRS)) < ROPE_PAIRS
    else:
        q_ref, k_ref, qw_ref, kw_ref, qo_ref, ko_ref, kf_ref = refs

    def rot(x):
        if not rope:
            return x
        partner = jnp.where(first_half, pltpu.roll(x, LANES - ROPE_PAIRS, axis=1), pltpu.roll(x, ROPE_PAIRS, axis=1))
        return x * cos + partner * sin

    for hd in range(H_ATT):
        cols = slice(hd * HEAD_DIM, (hd + 1) * HEAD_DIM)
        qo_ref[:, cols] = (rot(_rms_rows(q_ref[:, cols], qw_ref[...])) * Q_PRESCALE).astype(qo_ref.dtype)
    for hd in range(KV_HEADS):
        cols = slice(hd * HEAD_DIM, (hd + 1) * HEAD_DIM)
        kn = _rms_rows(k_ref[:, cols], kw_ref[...])
        if not rope:
            kf_ref[:, cols] = kn
        ko_ref[:, cols] = rot(kn).astype(ko_ref.dtype)


def _qk_prep(z3, q_norm_w, k_norm_w, layer, rope_tabs):
    B, L, _ = z3.shape
    tl = min(QK_ROWS, L)
    rope = rope_tabs is not None
    in_specs = [
        pl.BlockSpec((None, tl, D_ATT), lambda b, i: (b, i, COL_AQ)),
        pl.BlockSpec((None, tl, D_KV), lambda b, i: (b, i, COL_AK)),
        pl.BlockSpec((None, 1, HEAD_DIM), lambda b, i: (layer, 0, 0)),
        pl.BlockSpec((None, 1, HEAD_DIM), lambda b, i: (layer, 0, 0)),
    ]
    args = [z3, z3, q_norm_w.reshape(DEPTH, 1, HEAD_DIM), k_norm_w.reshape(DEPTH, 1, HEAD_DIM)]
    out_shape = [jax.ShapeDtypeStruct((B, L, D_ATT), BF16), jax.ShapeDtypeStruct((B, L, D_KV), BF16)]
    out_specs = [pl.BlockSpec((None, tl, D_ATT), lambda b, i: (b, i, 0)),
                 pl.BlockSpec((None, tl, D_KV), lambda b, i: (b, i, 0))]
    if rope:
        in_specs += [pl.BlockSpec((tl, HEAD_DIM), lambda b, i: (i, 0))] * 2
        args += list(rope_tabs)
    else:
        out_shape.append(jax.ShapeDtypeStruct((B, L, D_KV), F32))
        out_specs.append(pl.BlockSpec((None, tl, D_KV), lambda b, i: (b, i, 0)))
    return pl.pallas_call(
        functools.partial(_qk_kernel, rope=rope),
        out_shape=out_shape,
        grid=(B, L // tl),
        in_specs=in_specs,
        out_specs=out_specs,
        compiler_params=_cparams(("arbitrary", "arbitrary"), 40),
        name="qk_prep",
    )(*args)


def _rope_tables(n_tokens):
    rows = n_tokens // GRID_W
    row = jnp.broadcast_to(jnp.arange(rows, dtype=F32)[:, None], (rows, GRID_W)).reshape(-1)
    col = jnp.broadcast_to(jnp.arange(GRID_W, dtype=F32)[None, :], (rows, GRID_W)).reshape(-1)
    inv = ROPE_THETA ** (-2.0 * jnp.arange(ROPE_PAIRS, dtype=F32) / (2 * ROPE_PAIRS))
    ang_r = row[:, None] * inv
    ang_c = col[:, None] * inv
    cos = jnp.concatenate([jnp.cos(ang_r), jnp.cos(ang_r), jnp.cos(ang_c), jnp.cos(ang_c)], axis=1)
    sin = jnp.concatenate([-jnp.sin(ang_r), jnp.sin(ang_r), -jnp.sin(ang_c), jnp.sin(ang_c)], axis=1)
    return cos, sin


def _att_kernel(q_ref, k_ref, v1_ref, o_ref):
    k = k_ref[...]
    v1 = v1_ref[...]
    nt = (((1,), (1,)), ((), ()))
    for g in range(Q_PER_KV):
        cols = slice(g * HEAD_DIM, (g + 1) * HEAD_DIM)
        for r0 in range(0, q_ref.shape[0], ATT_UNIT_ROWS):
            rows = slice(r0, r0 + ATT_UNIT_ROWS)
            s = lax.dot_general(q_ref[rows, cols], k, nt, preferred_element_type=F32)
            p = jnp.exp2(s - jnp.max(s, axis=-1, keepdims=True)).astype(BF16)
            ol = jnp.dot(p, v1, preferred_element_type=F32)
            o_ref[rows, cols] = (ol[:, :HEAD_DIM] / ol[:, HEAD_DIM:]).astype(o_ref.dtype)


ATT_UNIT_ROWS = 256
ATT_UNITS_PER_HEAD = 4


def _attention(q, keys, vals):
    B, L, _ = q.shape
    Lk = keys.shape[1]
    tq = min(ATT_UNITS_PER_HEAD * ATT_UNIT_ROWS, L)
    blk = Q_PER_KV * HEAD_DIM
    v4 = vals.reshape(B, Lk, KV_HEADS, HEAD_DIM)
    vals1 = jnp.concatenate([v4, jnp.ones_like(v4)], axis=-1).reshape(B, Lk, 2 * D_KV)
    return pl.pallas_call(
        _att_kernel,
        out_shape=jax.ShapeDtypeStruct((B, L, D_ATT), BF16),
        grid=(B, KV_HEADS, L // tq),
        in_specs=[
            pl.BlockSpec((None, tq, blk), lambda b, h, i: (b, i, h)),
            pl.BlockSpec((None, Lk, HEAD_DIM), lambda b, h, i: (b, 0, h)),
            pl.BlockSpec((None, Lk, 2 * HEAD_DIM), lambda b, h, i: (b, 0, h)),
        ],
        out_specs=pl.BlockSpec((None, tq, blk), lambda b, h, i: (b, i, h)),
        compiler_params=_cparams(("arbitrary", "arbitrary", "arbitrary"), 48),
        name="attention",
    )(q, keys, vals1)


def _ln_kernel(*refs, n_y, gate_idx, mod_idx, with_router):
    x_ref = refs[0]
    y_refs = refs[1:1 + n_y]
    mg_ref, lnw_ref, lnb_ref = refs[1 + n_y:4 + n_y]
    rest = list(refs[4 + n_y:])
    y = y_refs[0][...]
    for r in y_refs[1:]:
        y = y + r[...]
    t = DEEPNORM_ALPHA * x_ref[...] + mg_ref[gate_idx:gate_idx + 1, :] * y
    xn = _normalize_rows(t) * lnw_ref[...] + lnb_ref[...]
    if mod_idx is None:
        (xo_ref,) = rest
        xo_ref[...] = xn
        return
    mh_ref = rest.pop(0)
    h = xn * (1.0 + mh_ref[mod_idx + 1:mod_idx + 2, :]) + mh_ref[mod_idx:mod_idx + 1, :]
    if with_router:
        wh_ref, wl_ref, br_ref, xo_ref, ho_ref, lg_ref = rest
        h_hi = h.astype(BF16)
        h_lo = (h - h_hi.astype(F32)).astype(BF16)
        w_hi = wh_ref[...]
        lg_ref[...] = (jnp.dot(h_hi, w_hi, preferred_element_type=F32)
                       + (jnp.dot(h_lo, w_hi, preferred_element_type=F32)
                          + jnp.dot(h_hi, wl_ref[...], preferred_element_type=F32))
                       + br_ref[...])
    else:
        xo_ref, ho_ref = rest
    xo_ref[...] = xn
    ho_ref[...] = h.astype(ho_ref.dtype)


def _ln_block(x, ys, mods_gate, gate_idx, ln_w, ln_b, layer, mods_h=None, mod_idx=None, router=None):
    B, L, D = x.shape
    tl = LN_ROWS
    row_spec = pl.BlockSpec((None, tl, D), lambda b, i: (b, i, 0))
    in_specs = [row_spec]
    args = [x]
    for arr, first in ys:
        in_specs.append(pl.BlockSpec((tl, D), functools.partial(lambda b, i, first: (first + b * (L // tl) + i, 0),
                                                                first=first)))
        args.append(arr)
    vec_spec = pl.BlockSpec((None, 1, D), lambda b, i: (layer, 0, 0))
    in_specs += [pl.BlockSpec((None, N_MOD, D), _mods_index(mods_gate.shape[0])), vec_spec, vec_spec]
    args += [mods_gate, ln_w.reshape(DEPTH, 1, D), ln_b.reshape(DEPTH, 1, D)]
    out_shape = [jax.ShapeDtypeStruct((B, L, D), F32)]
    out_specs = [row_spec]
    if mod_idx is not None:
        in_specs.append(pl.BlockSpec((None, N_MOD, D), _mods_index(mods_h.shape[0])))
        args.append(mods_h)
        if router is not None:
            w_hi, w_lo, b_r = router
            w_spec = pl.BlockSpec((None, D, ROUTER_PAD), lambda b, i: (layer, 0, 0))
            in_specs += [w_spec, w_spec, pl.BlockSpec((None, 1, ROUTER_PAD), lambda b, i: (layer, 0, 0))]
            args += [w_hi, w_lo, b_r]
            out_shape += [jax.ShapeDtypeStruct((B, L, D), F32), jax.ShapeDtypeStruct((B, L, ROUTER_PAD), F32)]
            out_specs += [row_spec, pl.BlockSpec((None, tl, ROUTER_PAD), lambda b, i: (b, i, 0))]
        else:
            out_shape.append(jax.ShapeDtypeStruct((B, L, D), BF16))
            out_specs.append(row_spec)
    return pl.pallas_call(
        functools.partial(_ln_kernel, n_y=len(ys), gate_idx=gate_idx, mod_idx=mod_idx,
                          with_router=router is not None),
        out_shape=out_shape,
        grid=(B, L // tl),
        in_specs=in_specs,
        out_specs=out_specs,
        compiler_params=_cparams(("arbitrary", "arbitrary"), 48),
        name="layernorm",
    )(*args)


MOE_BUFS = 3


def _up_kernel(be_ref, nu_ref, tok_ref, x_hbm, wg_ref, wu_ref, o_ref, xbuf, sem):
    b = pl.program_id(0)
    n_used = nu_ref[0]

    def row_copy(block, r):
        to = block % MOE_BUFS
        t = tok_ref[block * MOE_ROWS + r]
        return pltpu.make_async_copy(x_hbm.at[pl.ds(t, 1), :], xbuf.at[to, pl.ds(r, 1), :], sem.at[to])

    def issue_loop(block):
        def body(r, carry):
            row_copy(block, r).start()
            return carry

        lax.fori_loop(0, MOE_ROWS, body, 0)

    def wait_block(block):
        s = block % MOE_BUFS
        pltpu.make_async_copy(x_hbm.at[pl.ds(0, MOE_ROWS), :], xbuf.at[s], sem.at[s]).wait()

    @pl.when(b == 0)
    def _():
        issue_loop(0)

        @pl.when(1 < n_used)
        def _():
            issue_loop(1)

        wait_block(0)

    def compute(prefetch):
        x = xbuf[b % MOE_BUFS].astype(BF16)
        if prefetch:
            for r in range(MOE_ROWS):
                row_copy(b + 2, r).start()
        hg = jnp.dot(x, wg_ref[...], preferred_element_type=F32)
        hu = jnp.dot(x, wu_ref[...], preferred_element_type=F32)
        o_ref[...] = (_silu(hg) * hu).astype(o_ref.dtype)

    @pl.when(b + 2 < n_used)
    def _():
        compute(True)

    @pl.when((b < n_used) & (b + 2 >= n_used))
    def _():
        compute(False)

    @pl.when(b >= n_used)
    def _():
        o_ref[...] = jnp.zeros_like(o_ref)

    @pl.when(b + 1 < n_used)
    def _():
        wait_block(b + 1)


def _down_kernel(be_ref, nu_ref, dst_ref, a_ref, wd_ref, sw_ref, out_hbm, y0, y1, y2, sem, *, n_rows):
    b = pl.program_id(0)
    n_used = nu_ref[0]
    bufs = (y0, y1, y2)

    def row_copy(k, block, r):
        d = dst_ref[block * MOE_ROWS + r]
        return pltpu.make_async_copy(bufs[k].at[pl.ds(r, 1), :], out_hbm.at[pl.ds(d, 1), :], sem.at[k])

    def product():
        return jnp.dot(a_ref[...], wd_ref[...], preferred_element_type=F32) * sw_ref[...]

    @pl.when(b == 0)
    def _():
        for s in range(2):
            bufs[s][...] = jnp.zeros_like(bufs[s])
            fill = pltpu.make_async_copy(bufs[s], out_hbm.at[pl.ds(n_rows + s * MOE_ROWS, MOE_ROWS), :], sem.at[s])
            fill.start()
            fill.wait()
        y0[...] = product()

    for k in range(MOE_BUFS):
        prev = (k - 1) % MOE_BUFS

        @pl.when((b % MOE_BUFS == k) & (b >= 1) & (b < n_used))
        def _():
            for r in range(MOE_ROWS):
                row_copy(prev, b - 1, r).start()
            bufs[k][...] = product()

        @pl.when((b % MOE_BUFS == k) & (b == n_used))
        def _():
            def body(r, carry):
                row_copy(prev, b - 1, r).start()
                return carry

            lax.fori_loop(0, MOE_ROWS, body, 0)

    for k in range(MOE_BUFS):
        @pl.when(((b + 1) % MOE_BUFS == k) & (b >= 2) & (b - 2 < n_used))
        def _():
            pltpu.make_async_copy(bufs[k], out_hbm.at[pl.ds(0, MOE_ROWS), :], sem.at[k]).wait()


def _experts(h2, slot_tok, slot_dst, block_e, n_used, slot_w, w_up16, w_down16, layer):
    T = h2.shape[0]
    n_slots = slot_tok.shape[0]
    nb = n_slots // MOE_ROWS
    up_spec = pltpu.PrefetchScalarGridSpec(
        num_scalar_prefetch=3,
        grid=(nb,),
        in_specs=[
            pl.BlockSpec(memory_space=pl.ANY),
            pl.BlockSpec((None, None, D_MODEL, D_EXPERT), lambda b, be, nu, tok: (layer, be[b], 0, 0)),
            pl.BlockSpec((None, None, D_MODEL, D_EXPERT), lambda b, be, nu, tok: (layer, be[b], 0, 1)),
        ],
        out_specs=pl.BlockSpec((MOE_ROWS, D_EXPERT), lambda b, be, nu, tok: (b, 0)),
        scratch_shapes=[pltpu.VMEM((MOE_BUFS, MOE_ROWS, D_MODEL), F32), pltpu.SemaphoreType.DMA((MOE_BUFS,))],
    )
    act = pl.pallas_call(
        _up_kernel,
        out_shape=jax.ShapeDtypeStruct((n_slots, D_EXPERT), BF16),
        grid_spec=up_spec,
        compiler_params=_cparams(("arbitrary",), 56),
        name="expert_up",
    )(block_e, n_used, slot_tok, h2, w_up16, w_up16)
    down_spec = pltpu.PrefetchScalarGridSpec(
        num_scalar_prefetch=3,
        grid=(nb,),
        in_specs=[
            pl.BlockSpec((MOE_ROWS, D_EXPERT), lambda b, be, nu, dst: (b, 0)),
            pl.BlockSpec((None, None, D_EXPERT, D_MODEL), lambda b, be, nu, dst: (layer, be[b], 0, 0)),
            pl.BlockSpec((MOE_ROWS, 1), lambda b, be, nu, dst: (b, 0)),
        ],
        out_specs=pl.BlockSpec(memory_space=pl.ANY),
        scratch_shapes=[pltpu.VMEM((MOE_ROWS, D_MODEL), F32)] * MOE_BUFS + [pltpu.SemaphoreType.DMA((MOE_BUFS,))],
    )
    return pl.pallas_call(
        functools.partial(_down_kernel, n_rows=2 * T),
        out_shape=jax.ShapeDtypeStruct((2 * T + 2 * MOE_ROWS, D_MODEL), F32),
        grid_spec=down_spec,
        compiler_params=_cparams(("arbitrary",), 48),
        name="expert_down",
    )(block_e, n_used, slot_dst, act, w_down16, slot_w.reshape(n_slots, 1))


def _route(logits):
    T = logits.shape[0]
    lc = logits[:, :N_GROUPS]
    lf = logits[:, N_GROUPS:N_GROUPS + N_EXPERTS].reshape(T, N_GROUPS, EXPERTS_PER_GROUP)
    grp = jnp.argmax(lc, axis=-1).astype(jnp.int32)
    p_grp = jnp.take_along_axis(jax.nn.softmax(lc, axis=-1), grp[:, None], axis=-1)
    lfg = jnp.take_along_axis(lf, grp[:, None, None], axis=1)[:, 0]
    top_v, top_i = lax.top_k(lfg, 2)
    gate = jax.nn.softmax(top_v, axis=-1) * p_grp
    eid = (grp[:, None] * EXPERTS_PER_GROUP + top_i.astype(jnp.int32)).T.reshape(-1)
    wgt = gate.T.reshape(-1)
    n_assign = 2 * T
    experts = jnp.arange(N_EXPERTS, dtype=jnp.int32)
    counts = jnp.sum(eid[:, None] == experts[None, :], axis=0, dtype=jnp.int32)
    start = jnp.cumsum(counts) - counts
    pcounts = (counts + MOE_ROWS - 1) // MOE_ROWS * MOE_ROWS
    pend = jnp.cumsum(pcounts)
    pstart = pend - pcounts
    order = jnp.argsort(eid).astype(jnp.int32)
    n_blocks = n_assign // MOE_ROWS + N_EXPERTS + 2
    block_first = jnp.arange(n_blocks, dtype=jnp.int32) * MOE_ROWS
    block_e = jnp.minimum(jnp.sum(pend[None, :] <= block_first[:, None], axis=1, dtype=jnp.int32), N_EXPERTS - 1)
    r = (block_first - pstart[block_e])[:, None] + jnp.arange(MOE_ROWS, dtype=jnp.int32)[None, :]
    valid = r < counts[block_e][:, None]
    src = order[jnp.clip(start[block_e][:, None] + r, 0, n_assign - 1)]
    in_block = jnp.arange(MOE_ROWS, dtype=jnp.int32)[None, :]
    slot = block_first[:, None] + in_block
    spare = n_assign + (jnp.arange(n_blocks, dtype=jnp.int32) % 2)[:, None] * MOE_ROWS + in_block
    slot_tok = jnp.where(valid, src % T, slot % T).reshape(-1)
    slot_dst = jnp.where(valid, src, spare).reshape(-1)
    slot_w = jnp.where(valid, wgt[src], 0.0).reshape(-1)
    n_used = (pend[-1:] // MOE_ROWS).astype(jnp.int32)
    return slot_tok, slot_dst, slot_w, block_e, n_used


def _layer(x, h, mods, mods_next, W, layer, ctx, rope_tabs):
    B, L, D = x.shape
    T = B * L
    z = _matmul([h.reshape(T, D)], W['w_in'], layer, (D,), F32)
    z3 = z.reshape(B, L, N_IN)

    ret, s_new = _retention(z3, W['log_decay'], ctx[0] if ctx else None, W['ret_gn_w'], layer, ctx is not None)
    sm = _spatial_gating(z3, W['sm_ln_w'], W['sm_w'], W['sm_bT'], layer)
    v_new = z3[:, :, OFF_AV:OFF_AV + D_KV]
    if ctx is None:
        q16, k16, k_new = _qk_prep(z3, W['q_norm_w'], W['k_norm_w'], layer, None)
        keys, vals = k16, v_new.astype(BF16)
        new_ctx = (s_new, k_new.reshape(B, L, KV_HEADS, HEAD_DIM), v_new.reshape(B, L, KV_HEADS, HEAD_DIM))
    else:
        q16, k16 = _qk_prep(z3, W['q_norm_w'], W['k_norm_w'], layer, rope_tabs)
        keys = jnp.concatenate([k16, ctx[1]], axis=1)
        vals = jnp.concatenate([v_new.astype(BF16), ctx[2]], axis=1)
        new_ctx = None
    att = _attention(q16, keys, vals)

    mix = _matmul([ret.reshape(T, D_RET), sm.reshape(T, D_SM), att.reshape(T, D_ATT)], W['w_out'], layer,
                  (D_RET, D_SM, D_ATT), F32)
    x1, h2, logits = _ln_block(x, [(mix, 0)], mods, 2, W['ln1_w'], W['ln1_b'], layer,
                               mods_h=mods, mod_idx=3, router=(W['router_w_hi'], W['router_w_lo'], W['router_b']))

    slot_tok, slot_dst, slot_w, block_e, n_used = _route(logits.reshape(T, ROUTER_PAD))
    ffn = _experts(h2.reshape(T, D), slot_tok, slot_dst, block_e, n_used, slot_w, W['w_up'], W['w_down'], layer)
    ys = [(ffn, 0), (ffn, T // LN_ROWS)]
    if mods_next is None:
        (x2,) = _ln_block(x1, ys, mods, 5, W['ln2_w'], W['ln2_b'], layer)
        h_next = None
    else:
        x2, h_next = _ln_block(x1, ys, mods, 5, W['ln2_w'], W['ln2_b'], layer, mods_h=mods_next, mod_idx=0)
    return x2, h_next, new_ctx


def kernel(x_prompt, x_sample, state_ret, cache_k, cache_v, c, c_ctx, w_mod, b_mod, w_in, ret_log_decay, ret_gn_w, sm_ln_w, sm_w, sm_b, q_norm_w, k_norm_w, w_out, ln1_w, ln1_b, router_w_coarse, router_b_coarse, router_w_fine, router_b_fine, w_up, w_down, ln2_w, ln2_b):
    n_dec = c.shape[0]
    cond = jnp.zeros((16, D_MODEL), F32).at[:n_dec].set(c).at[n_dec].set(c_ctx)
    mods_all = _adaln_mods(cond, w_mod, b_mod)
    mods_lat = [mods_all[l, :n_dec].reshape(n_dec, N_MOD, D_MODEL) for l in range(DEPTH)]
    mods_ctx = [mods_all[l, n_dec:n_dec + 1].reshape(1, N_MOD, D_MODEL) for l in range(DEPTH)]

    n_r = N_GROUPS + N_EXPERTS
    router_w = jnp.zeros((DEPTH, D_MODEL, ROUTER_PAD), F32).at[:, :, :n_r].set(
        jnp.concatenate([router_w_coarse, router_w_fine], axis=-1))
    router_b = jnp.zeros((DEPTH, 1, ROUTER_PAD), F32).at[:, 0, :n_r].set(
        jnp.concatenate([router_b_coarse, router_b_fine], axis=-1))
    router_w_hi = router_w.astype(BF16)
    W = {
        'w_in': w_in.astype(BF16), 'w_out': w_out.astype(BF16),
        'w_up': w_up.astype(BF16), 'w_down': w_down.astype(BF16),
        'log_decay': ret_log_decay.reshape(-1), 'ret_gn_w': ret_gn_w,
        'sm_ln_w': sm_ln_w, 'sm_w': sm_w.astype(BF16), 'sm_bT': jnp.swapaxes(sm_b, 1, 2),
        'q_norm_w': q_norm_w, 'k_norm_w': k_norm_w,
        'ln1_w': ln1_w, 'ln1_b': ln1_b, 'ln2_w': ln2_w, 'ln2_b': ln2_b,
        'router_w_hi': router_w_hi, 'router_w_lo': (router_w - router_w_hi.astype(F32)).astype(BF16),
        'router_b': router_b,
    }
    rope_tabs = _rope_tables(x_sample.shape[1])
    past = cache_k.shape[2]
    ck16 = cache_k.astype(BF16).reshape(n_dec, DEPTH, past, D_KV)
    cv16 = cache_v.astype(BF16).reshape(n_dec, DEPTH, past, D_KV)

    y_p, y_s = x_prompt, x_sample
    h_p = _modulate(x_prompt, mods_ctx[0])
    h_s = _modulate(x_sample, mods_lat[0])
    states, ctx_k, ctx_v = [], [], []
    for l in range(DEPTH):
        nxt = l + 1 < DEPTH
        y_p, h_p, (s_l, k_l, v_l) = _layer(y_p, h_p, mods_ctx[l], mods_ctx[l + 1] if nxt else None, W, l, None, None)
        states.append(s_l)
        ctx_k.append(k_l)
        ctx_v.append(v_l)
        y_s, h_s, _ = _layer(y_s, h_s, mods_lat[l], mods_lat[l + 1] if nxt else None, W, l,
                             (state_ret, ck16[:, l], cv16[:, l]), rope_tabs)
    return (y_p, y_s, jnp.stack(states, axis=1), jnp.stack(ctx_k, axis=1), jnp.stack(ctx_v, axis=1))
```
